```python
import math
import jax, jax.numpy as jnp
from jax import lax
import numpy as np

D_MODEL = 1024
BATCH = 2
SEQ = 8192
DEPTH = 1

N_META = 16
EPS = 1e-6
GLA_HEADS = 4
GLA_DK = 128
GLA_DV = 256
GLA_LOWRANK = 16
GLA_TAU = 16.0
GLA_CHUNK = 64
DSA_HEADS = 8
DSA_DH = 128
IDX_HEADS = 8
IDX_DH = 64
TOPK_MAX = 256
Q_BLOCK = 64
REL_BUCKETS = 32
REL_MAX_DIST = 128
D_FF = 2816

IN_SIZES = (GLA_HEADS * GLA_DK, GLA_HEADS * GLA_DK, GLA_HEADS * GLA_DV, GLA_HEADS * GLA_DV, GLA_LOWRANK,
            DSA_HEADS * DSA_DH, DSA_HEADS * DSA_DH, DSA_HEADS * DSA_DH,
            IDX_HEADS * IDX_DH, IDX_DH, IDX_HEADS, D_MODEL, D_MODEL)
IN_WIDTH = sum(IN_SIZES)

kernel_name = 'gla_dsa_macaron_hybrid'


def rmsnorm(x, g):
    x32 = x.astype(jnp.float32)
    y = x32 * lax.rsqrt(jnp.mean(x32 * x32, axis=-1, keepdims=True) + EPS)
    return (y * g.astype(jnp.float32)).astype(x.dtype)


def swiglu(x, w_gate, w_up, w_down):
    return (jax.nn.silu(x @ w_gate) * (x @ w_up)) @ w_down


def rel_bucket(dist):
    n = jnp.maximum(dist, 0)
    max_exact = REL_BUCKETS // 2
    nf = jnp.maximum(n, max_exact).astype(jnp.float32)
    large = max_exact + (jnp.log(nf / max_exact) / math.log(REL_MAX_DIST / max_exact)
                         * (REL_BUCKETS - max_exact)).astype(jnp.int32)
    large = jnp.minimum(large, REL_BUCKETS - 1)
    return jnp.where(n < max_exact, n, large)


def split_in(z):
    idx = np.cumsum(np.array(IN_SIZES))[:-1].tolist()
    return jnp.split(z, idx, axis=-1)


def gla_mixer(q, k, v, r, lr, w_lr, b_lr, g_norm):
    B, T, _ = q.shape
    H, dk, dv, C = GLA_HEADS, GLA_DK, GLA_DV, GLA_CHUNK
    log_a = jax.nn.log_sigmoid((lr @ w_lr + b_lr).astype(jnp.float32)) / GLA_TAU
    pad = (-N_META) % C
    Tp = T + pad
    n = Tp // C
    def chunks(a, d):
        a = jnp.pad(a.astype(jnp.float32), ((0, 0), (pad, 0), (0, 0)))
        return a.reshape(B, n, C, H, d).transpose(1, 0, 3, 2, 4)
    qc = chunks(q, dk) * (dk ** -0.5)
    kc = chunks(k, dk)
    vc = chunks(v, dv)
    gc = chunks(log_a, dk)
    causal = jnp.tril(jnp.ones((C, C), dtype=bool))

    def step(S, inp):
        qi, ki, vi, gi = inp
        b = jnp.cumsum(gi, axis=2)
        diff = b[:, :, :, None, :] - b[:, :, None, :, :]
        decay = jnp.exp(jnp.where(causal[None, None, :, :, None], diff, -jnp.inf))
        A = jnp.einsum('bhtd,bhsd,bhtsd->bhts', qi, ki, decay)
        o = jnp.einsum('bhts,bhsv->bhtv', A, vi) + jnp.einsum('bhtd,bhdv->bhtv', qi * jnp.exp(b), S)
        b_last = b[:, :, -1:, :]
        S = jnp.exp(b_last[:, :, 0, :])[..., None] * S + jnp.einsum('bhsd,bhsv->bhdv', ki * jnp.exp(b_last - b), vi)
        return S, o

    S0 = jnp.zeros((B, H, dk, dv), jnp.float32)
    _, o = lax.scan(step, S0, (qc, kc, vc, gc))
    o = o.transpose(1, 0, 3, 2, 4).reshape(B, Tp, H, dv)[:, pad:].astype(v.dtype)
    o = rmsnorm(o, g_norm)
    return o.reshape(B, T, H * dv) * jax.nn.silu(r)


def dsa_mixer(q, k, v, iq, ik, iw, rel_bias):
    B, T, _ = q.shape
    H, dh, Hi, di, QB = DSA_HEADS, DSA_DH, IDX_HEADS, IDX_DH, Q_BLOCK
    L = T - N_META
    topk = min(TOPK_MAX, L // 4)
    q = q.reshape(B, T, H, dh) * (dh ** -0.5)
    k = k.reshape(B, T, H, dh)
    v = v.reshape(B, T, H, dh)
    iq = iq.reshape(B, T, Hi, di)
    iw = iw * ((Hi ** -0.5) * (di ** -0.5))
    n_blk = -(-T // QB)
    Tp = n_blk * QB
    def blocks(a):
        a = jnp.pad(a, ((0, 0), (0, Tp - T)) + ((0, 0),) * (a.ndim - 2))
        return jnp.moveaxis(a.reshape((B, n_blk, QB) + a.shape[2:]), 1, 0)
    key_pos = jnp.arange(T, dtype=jnp.int32)
    starts = jnp.arange(n_blk, dtype=jnp.int32) * QB

    def block(args):
        qb, iqb, iwb, start = args
        qpos = start + jnp.arange(QB, dtype=jnp.int32)
        s = jnp.einsum('bqhd,bkd->bqhk', iqb, ik)
        score = jnp.einsum('bqh,bqhk->bqk', iwb, jax.nn.relu(s)).astype(jnp.float32)
        admissible = key_pos[None, :] <= qpos[:, None]
        score = jnp.where(admissible[None], score, -jnp.inf)
        _, idx = lax.top_k(score, topk)
        kg = jax.vmap(lambda kk, ii: kk[ii])(k, idx)
        vg = jax.vmap(lambda vv, ii: vv[ii])(v, idx)
        logits = jnp.einsum('bqhd,bqkhd->bhqk', qb, kg).astype(jnp.float32)
        dist = qpos[None, :, None] - idx
        bias = rel_bias[rel_bucket(dist)].astype(jnp.float32)
        logits = logits + jnp.transpose(bias, (0, 3, 1, 2))
        valid = idx <= qpos[None, :, None]
        logits = jnp.where(valid[:, None], logits, -jnp.inf)
        p = jax.nn.softmax(logits, axis=-1).astype(v.dtype)
        return jnp.einsum('bhqk,bqkhd->bqhd', p, vg)

    out = lax.map(block, (blocks(q), blocks(iq), blocks(iw), starts))
    return jnp.moveaxis(out, 0, 1).reshape(B, Tp, H * dh)[:, :T]


def setup_inputs(seed: int = 0) -> dict:
    key = jax.random.key(seed)
    ks = jax.random.split(key, 24)
    f32 = jnp.float32
    def nrm(k_, shape, scale):
        return jax.random.normal(k_, shape, f32) * scale
    def gain(k_, shape):
        return 1.0 + 0.02 * jax.random.normal(k_, shape, f32)
    Dl = DEPTH
    return {
        'x': jax.random.normal(ks[0], (BATCH, SEQ, D_MODEL), f32),
        'meta_tokens': nrm(ks[1], (N_META, D_MODEL), 1.0),
        'norm_ffn1': gain(ks[2], (Dl, D_MODEL)),
        'ffn1_w_gate': nrm(ks[3], (Dl, D_MODEL, D_FF), D_MODEL ** -0.5),
        'ffn1_w_up': nrm(ks[4], (Dl, D_MODEL, D_FF), D_MODEL ** -0.5),
        'ffn1_w_down': nrm(ks[5], (Dl, D_FF, D_MODEL), D_FF ** -0.5),
        'norm_mix': gain(ks[6], (Dl, D_MODEL)),
        'w_in': nrm(ks[7], (Dl, D_MODEL, IN_WIDTH), D_MODEL ** -0.5),
        'gla_w_lr': nrm(ks[8], (Dl, GLA_LOWRANK, GLA_HEADS * GLA_DK), GLA_LOWRANK ** -0.5),
        'gla_b_lr': nrm(ks[9], (Dl, GLA_HEADS * GLA_DK), 0.1),
        'gla_norm': gain(ks[10], (Dl, GLA_DV)),
        'w_up_a': nrm(ks[11], (Dl, GLA_HEADS * GLA_DV, D_MODEL), (GLA_HEADS * GLA_DV) ** -0.5),
        'w_up_b': nrm(ks[12], (Dl, DSA_HEADS * DSA_DH, D_MODEL), (DSA_HEADS * DSA_DH) ** -0.5),
        'w_out': nrm(ks[13], (Dl, D_MODEL, D_MODEL), D_MODEL ** -0.5),
        'rel_bias': nrm(ks[14], (REL_BUCKETS, DSA_HEADS), 0.5),
        'norm_ffn2': gain(ks[15], (Dl, D_MODEL)),
        'ffn2_w_gate': nrm(ks[16], (Dl, D_MODEL, D_FF), D_MODEL ** -0.5),
        'ffn2_w_up': nrm(ks[17], (Dl, D_MODEL, D_FF), D_MODEL ** -0.5),
        'ffn2_w_down': nrm(ks[18], (Dl, D_FF, D_MODEL), D_FF ** -0.5),
        'norm_final': gain(ks[19], (D_MODEL,)),
    }


def reference(x, meta_tokens, norm_ffn1, ffn1_w_gate, ffn1_w_up, ffn1_w_down, norm_mix, w_in,
              gla_w_lr, gla_b_lr, gla_norm, w_up_a, w_up_b, w_out, rel_bias, norm_ffn2,
              ffn2_w_gate, ffn2_w_up, ffn2_w_down, norm_final):
    B = x.shape[0]
    meta = jnp.broadcast_to(meta_tokens[None].astype(x.dtype), (B, N_META, D_MODEL))
    h = jnp.concatenate([meta, x], axis=1)
    for l in range(DEPTH):
        h = h + 0.5 * swiglu(rmsnorm(h, norm_ffn1[l]), ffn1_w_gate[l], ffn1_w_up[l], ffn1_w_down[l])
        u = rmsnorm(h, norm_mix[l])
        (gq, gk, gv, gr, glr, dq, dk_, dv_, iq, ik, iw, ga, gb) = split_in(u @ w_in[l])
        y_a = gla_mixer(gq, gk, gv, gr, glr, gla_w_lr[l], gla_b_lr[l], gla_norm[l])
        y_b = dsa_mixer(dq, dk_, dv_, iq, ik, iw, rel_bias)
        merged = jax.nn.sigmoid(ga) * (y_a @ w_up_a[l]) + jax.nn.sigmoid(gb) * (y_b @ w_up_b[l])
        h = h + merged @ w_out[l]
        h = h + 0.5 * swiglu(rmsnorm(h, norm_ffn2[l]), ffn2_w_gate[l], ffn2_w_up[l], ffn2_w_down[l])
    h = rmsnorm(h, norm_final)
    return h[:, N_META:]
```

```python
import functools

import numpy as np
import jax
import jax.numpy as jnp
from jax import lax
from jax.experimental import pallas as pl
from jax.experimental.pallas import tpu as pltpu

F32 = jnp.float32
BF16 = jnp.bfloat16
I32 = jnp.int32
HIGHEST = lax.Precision.HIGHEST

N_META = 16
EPS = 1e-6
GLA_HEADS, GLA_DK, GLA_DV, GLA_LOWRANK, GLA_TAU = 4, 128, 256, 16, 16.0
DSA_HEADS, DSA_DH, IDX_HEADS, IDX_DH, TOPK_MAX = 8, 128, 8, 64, 256
REL_BUCKETS, REL_MAX_DIST = 32, 128

LANES = 128
VMEM_LIMIT = 56 * 1024 * 1024
INT_MIN = -(2 ** 31)

ZG_W = 2 * GLA_HEADS * GLA_DK + 2 * GLA_HEADS * GLA_DV
ZD_W = 3 * DSA_HEADS * DSA_DH
SMALL_W = LANES
IQ_W = IDX_HEADS * IDX_DH
IK_OFF, IW_OFF, LR_OFF = 0, IDX_DH, IDX_DH + IDX_HEADS


def _rel_bucket_thresholds():
    n = np.arange(0, 4 * REL_MAX_DIST)
    max_exact = REL_BUCKETS // 2
    nf = np.maximum(n, max_exact).astype(np.float32)
    large = max_exact + (np.log(nf / np.float32(max_exact)) / np.float32(np.log(REL_MAX_DIST / max_exact))
                         * np.float32(REL_BUCKETS - max_exact)).astype(np.int32)
    large64 = max_exact + (np.log(np.maximum(n, max_exact) / max_exact) / np.log(REL_MAX_DIST / max_exact)
                           * (REL_BUCKETS - max_exact)).astype(np.int64)
    assert (large == large64).all()
    bucket = np.where(n < max_exact, n, np.minimum(large, REL_BUCKETS - 1))
    assert (np.diff(bucket) >= 0).all()
    return [int(np.argmax(bucket >= b)) for b in range(REL_BUCKETS)]


BUCKET_START = _rel_bucket_thresholds()
FAR_DIST = BUCKET_START[REL_BUCKETS - 1]


def _rms(x, g):
    return x * lax.rsqrt(jnp.mean(x * x, axis=-1, keepdims=True) + EPS) * g


def _sigmoid(x):
    return 1.0 / (1.0 + jnp.exp(-x))


def _params(*sem):
    return pltpu.CompilerParams(dimension_semantics=sem, vmem_limit_bytes=VMEM_LIMIT)


def _ffn_kernel(*refs, final_norm):
    if final_norm:
        x_ref, g_ref, wg_ref, wu_ref, wd_ref, gf_ref, o_ref, xn_ref = refs
    else:
        x_ref, g_ref, wg_ref, wu_ref, wd_ref, o_ref, xn_ref = refs
    j = pl.program_id(1)

    @pl.when(j == 0)
    def _():
        xn_ref[...] = _rms(x_ref[...], g_ref[...]).astype(BF16)
        o_ref[...] = jnp.zeros_like(o_ref)

    xn = xn_ref[...]
    a = jnp.dot(xn, wg_ref[...], preferred_element_type=F32)
    b = jnp.dot(xn, wu_ref[...], preferred_element_type=F32)
    hmid = (a * _sigmoid(a) * b).astype(BF16)
    o_ref[...] += jnp.dot(hmid, wd_ref[...], preferred_element_type=F32)

    @pl.when(j == pl.num_programs(1) - 1)
    def _():
        h = x_ref[...] + 0.5 * o_ref[...]
        if final_norm:
            h = _rms(h, gf_ref[...])
        o_ref[...] = h


def _ffn(x, g, wg, wu, wd, gf=None, *, tm, tf):
    n, d = x.shape
    dff = wg.shape[1]
    tm = min(tm, n)
    row = pl.BlockSpec((tm, d), lambda i, j: (i, 0))
    vec = pl.BlockSpec((1, d), lambda i, j: (0, 0))
    in_specs = [row, vec,
                pl.BlockSpec((d, tf), lambda i, j: (0, j)),
                pl.BlockSpec((d, tf), lambda i, j: (0, j)),
                pl.BlockSpec((tf, d), lambda i, j: (j, 0))]
    args = [x, g, wg, wu, wd]
    if gf is not None:
        in_specs.append(vec)
        args.append(gf)
    return pl.pallas_call(
        functools.partial(_ffn_kernel, final_norm=gf is not None),
        grid=(n // tm, dff // tf),
        in_specs=in_specs,
        out_specs=row,
        out_shape=jax.ShapeDtypeStruct((n, d), F32),
        scratch_shapes=[pltpu.VMEM((tm, d), BF16)],
        compiler_params=_params("parallel", "arbitrary"),
        name="ffn_final" if gf is not None else "ffn",
    )(*args)


def _proj_kernel(x_ref, g_ref, w_ref, o_ref, xn_ref):
    @pl.when(pl.program_id(1) == 0)
    def _():
        xn_ref[...] = _rms(x_ref[...], g_ref[...]).astype(BF16)

    o_ref[...] = jnp.dot(xn_ref[...], w_ref[...], preferred_element_type=F32).astype(o_ref.dtype)


def _proj(x, g, w, out_dtype, *, tm, tn, name):
    n, d = x.shape
    width = w.shape[1]
    tm = min(tm, n)
    tn = min(tn, width)
    return pl.pallas_call(
        _proj_kernel,
        grid=(n // tm, width // tn),
        in_specs=[pl.BlockSpec((tm, d), lambda i, j: (i, 0)),
                  pl.BlockSpec((1, d), lambda i, j: (0, 0)),
                  pl.BlockSpec((d, tn), lambda i, j: (0, j))],
        out_specs=pl.BlockSpec((tm, tn), lambda i, j: (i, j)),
        out_shape=jax.ShapeDtypeStruct((n, width), out_dtype),
        scratch_shapes=[pltpu.VMEM((tm, d), BF16)],
        compiler_params=_params("parallel", "arbitrary"),
        name=name,
    )(x, g, w)


def _lower_tri(n):
    r = lax.broadcasted_iota(I32, (n, n), 0)
    c = lax.broadcasted_iota(I32, (n, n), 1)
    return c <= r


def _gla_kernel(zg_ref, zs_ref, zgm_ref, zsm_ref, wlr_ref, blr_ref, gn_ref, ya_ref, st_ref, *, nb, chunk):
    H, dk, dv = GLA_HEADS, GLA_DK, GLA_DV
    kw = H * dk
    wlr = wlr_ref[...]
    blr = blr_ref[...]
    gn = gn_ref[...]

    def log_decay(zs):
        x = jnp.dot(zs, wlr, preferred_element_type=F32, precision=HIGHEST) + blr
        return (jnp.minimum(x, 0.0) - jnp.log(1.0 + jnp.exp(-jnp.abs(x)))) * (1.0 / GLA_TAU)

    def cumsum_rows(g):
        tri = _lower_tri(g.shape[0]).astype(F32)
        return jnp.dot(tri, g, preferred_element_type=F32, precision=HIGHEST)

    @pl.when(pl.program_id(0) == 0)
    def _():
        bc = cumsum_rows(log_decay(zsm_ref[...]))
        bl = bc[N_META - 1:N_META, :]
        zgm = zgm_ref[...]
        kl = (zgm[:, kw:2 * kw].astype(F32) * jnp.exp(bl - bc)).astype(BF16)
        v = zgm[:, 2 * kw:2 * kw + H * dv]
        for h in range(H):
            st = lax.dot_general(v[:, h * dv:(h + 1) * dv], kl[:, h * dk:(h + 1) * dk],
                                 (((0,), (0,)), ((), ())), preferred_element_type=F32)
            for b in range(nb):
                st_ref[b * H + h] = st

    causal = _lower_tri(chunk)
    mid = chunk // 2
    for b in range(nb):
        zg = zg_ref[b]
        bc = cumsum_rows(log_decay(zs_ref[b]))
        bl = bc[chunk - 1:chunk, :]
        bm = bc[mid - 1:mid, :]
        qe = zg[:, :kw].astype(F32) * (dk ** -0.5) * jnp.exp(bc - bm)
        ke = zg[:, kw:2 * kw].astype(F32) * jnp.exp(bm - bc)
        qs = (qe * jnp.exp(bm)).astype(BF16)
        kl = (ke * jnp.exp(bl - bm)).astype(BF16)
        qe = qe.astype(BF16)
        ke = ke.astype(BF16)
        dec = jnp.exp(bl)
        for h in range(H):
            ks = slice(h * dk, (h + 1) * dk)
            v = zg[:, 2 * kw + h * dv:2 * kw + (h + 1) * dv]
            r = zg[:, 2 * kw + H * dv + h * dv:2 * kw + H * dv + (h + 1) * dv].astype(F32)
            a = lax.dot_general(qe[:, ks], ke[:, ks], (((1,), (1,)), ((), ())), preferred_element_type=F32)
            a = jnp.where(causal, a, 0.0).astype(BF16)
            st = st_ref[b * H + h]
            o = jnp.dot(a, v, preferred_element_type=F32)
            o += lax.dot_general(qs[:, ks], st.astype(BF16), (((1,), (1,)), ((), ())), preferred_element_type=F32)
            st_ref[b * H + h] = st * dec[:, ks] + lax.dot_general(
                v, kl[:, ks], (((0,), (0,)), ((), ())), preferred_element_type=F32)
            y = _rms(o, gn) * (r * _sigmoid(r))
            ya_ref[b, :, h * dv:(h + 1) * dv] = y.astype(ya_ref.dtype)


def _gla(z3, zi3, z_meta, zi_meta, wlr_pad, blr, gn, *, chunk):
    nb, seq, _ = z3.shape
    H, dk, dv = GLA_HEADS, GLA_DK, GLA_DV
    small_blk = IQ_W // SMALL_W
    return pl.pallas_call(
        functools.partial(_gla_kernel, nb=nb, chunk=chunk),
        grid=(seq // chunk,),
        in_specs=[pl.BlockSpec((nb, chunk, ZG_W), lambda c: (0, c, 0)),
                  pl.BlockSpec((nb, chunk, SMALL_W), lambda c: (0, c, small_blk)),
                  pl.BlockSpec((N_META, ZG_W), lambda c: (0, 0)),
                  pl.BlockSpec((N_META, SMALL_W), lambda c: (0, small_blk)),
                  pl.BlockSpec((SMALL_W, H * dk), lambda c: (0, 0)),
                  pl.BlockSpec((1, H * dk), lambda c: (0, 0)),
                  pl.BlockSpec((1, dv), lambda c: (0, 0))],
        out_specs=pl.BlockSpec((nb, chunk, H * dv), lambda c: (0, c, 0)),
        out_shape=jax.ShapeDtypeStruct((nb, seq, H * dv), BF16),
        scratch_shapes=[pltpu.VMEM((nb * H, dv, dk), F32)],
        compiler_params=_params("arbitrary"),
        name="gla",
    )(z3, zi3, z_meta, zi_meta, wlr_pad, blr, gn)


def _sort_key(score):
    bits = pltpu.bitcast(score, I32)
    return bits ^ (lax.shift_right_arithmetic(bits, 31) & 0x7FFFFFFF)


def _dsa_kernel(q_ref, k_ref, v_ref, km_ref, vm_ref, zi_ref, ikt_ref, relb_ref, yb_ref,
                skey_ref, thr_ref, qs_ref, m_ref, l_ref, acc_ref, bias_ref, *, tq, topk, sel_rows):
    H, dh = DSA_HEADS, DSA_DH
    b_id, qi, kb = pl.program_id(0), pl.program_id(1), pl.program_id(2)
    n_chunks = qi + 2
    row = lax.broadcasted_iota(I32, (tq, tq), 0)
    col = lax.broadcasted_iota(I32, (tq, tq), 1)

    @pl.when((b_id == 0) & (qi == 0) & (kb == 0))
    def _():
        for t, dist in enumerate((N_META + row - col, row - col, tq + row - col)):
            for h in range(H):
                far = relb_ref[REL_BUCKETS - 1, h]
                tile = jnp.full((tq, tq), relb_ref[0, h] - far, F32)
                for bkt in range(1, REL_BUCKETS):
                    tile = jnp.where(dist >= BUCKET_START[bkt], relb_ref[bkt, h] - far, tile)
                bias_ref[t, h] = tile

    @pl.when(kb == 0)
    def _():
        zi = zi_ref[0]
        qs_ref[...] = (q_ref[0] * (dh ** -0.5)).astype(BF16)
        m_ref[...] = jnp.full(m_ref.shape, -1e30, F32)
        l_ref[...] = jnp.zeros(l_ref.shape, F32)
        acc_ref[...] = jnp.zeros(acc_ref.shape, F32)

        iq = [zi[:, h * IDX_DH:(h + 1) * IDX_DH].astype(BF16) for h in range(IDX_HEADS)]
        wscale = (IDX_HEADS ** -0.5) * (IDX_DH ** -0.5)
        iw = [zi[:, IQ_W + IW_OFF + h:IQ_W + IW_OFF + h + 1] * wscale for h in range(IDX_HEADS)]

        def score_keys(ci):
            ikc = ikt_ref[0, ci]
            sc = jnp.zeros((tq, tq), F32)
            for h in range(IDX_HEADS):
                s = jnp.dot(iq[h], ikc, preferred_element_type=F32)
                sc += iw[h] * jnp.maximum(s, 0.0)
            return _sort_key(sc)

        skey_ref[0] = jnp.where(col < N_META, score_keys(0), INT_MIN)

        def full_chunk(ci, carry):
            skey_ref[ci] = score_keys(ci)
            return carry

        lax.fori_loop(1, qi + 1, full_chunk, 0)
        skey_ref[qi + 1] = jnp.where(col <= row, score_keys(qi + 1), INT_MIN)

        for rg in range(tq // sel_rows):
            rows = slice(rg * sel_rows, (rg + 1) * sel_rows)

            def count_ge(cand):
                def body(ci, acc):
                    ge = (skey_ref[ci, rows, :] >= cand).astype(I32)
                    for t in range(tq // LANES):
                        acc += ge[:, t * LANES:(t + 1) * LANES]
                    return acc
                acc = lax.fori_loop(0, n_chunks, body, jnp.zeros((sel_rows, LANES), I32))
                return jnp.sum(acc, axis=1, keepdims=True)

            def search(p, prefix):
                cand = prefix ^ lax.shift_left(jnp.int32(1), 31 - p)
                return jnp.where(count_ge(cand) >= topk, cand, prefix)

            prefix = lax.fori_loop(0, 32, search, jnp.full((sel_rows, 1), INT_MIN, I32))
            thr = jnp.maximum(prefix, INT_MIN + 1)
            thr_ref[rows, :] = thr

            n_ge = count_ge(thr)
            n_gt = count_ge(thr + 1)

            @pl.when(jnp.max(n_ge) > topk)
            def _():
                need = (topk - n_gt).astype(F32)
                incl = (col <= row).astype(BF16)

                def fix(ci, seen):
                    sk = skey_ref[ci, rows, :]
                    tie = sk == thr
                    tie_f = jnp.where(tie, 1.0, 0.0)
                    run = lax.dot_general(tie_f.astype(BF16), incl, (((1,), (1,)), ((), ())),
                                          preferred_element_type=F32)
                    drop = tie & (seen + run > need)
                    skey_ref[ci, rows, :] = jnp.where(drop, INT_MIN, sk)
                    return seen + jnp.sum(tie_f, axis=1, keepdims=True)

                lax.fori_loop(0, n_chunks, fix, jnp.zeros((sel_rows, 1), F32))

    def attend(kt_ref, vt_ref, chunk, bias_idx):
        mask = skey_ref[chunk] >= thr_ref[...]
        for h in range(H):
            hs = slice(h * dh, (h + 1) * dh)
            s = lax.dot_general(qs_ref[:, hs], kt_ref[:, hs], (((1,), (1,)), ((), ())),
                                preferred_element_type=F32)
            if bias_idx is not None:
                s += bias_ref[bias_idx, h]
            s = jnp.where(mask, s, -jnp.inf)
            m_old = m_ref[h]
            m_new = jnp.maximum(m_old, jnp.max(s, axis=1, keepdims=True))
            alpha = jnp.exp(m_old - m_new)
            p = jnp.exp(s - m_new)
            l_ref[h] = alpha * l_ref[h] + jnp.sum(p, axis=1, keepdims=True)
            acc_ref[h] = alpha * acc_ref[h] + jnp.dot(p.astype(BF16), vt_ref[:, hs],
                                                      preferred_element_type=F32)
            m_ref[h] = m_new

    kx = k_ref.at[0]
    vx = v_ref.at[0]

    @pl.when((kb == 0) & (qi == 0))
    def _():
        attend(km_ref, vm_ref, 0, 0)

    @pl.when((kb == 0) & (qi > 0))
    def _():
        attend(km_ref, vm_ref, 0, None)

    @pl.when((kb >= 1) & (kb < qi))
    def _():
        attend(kx, vx, kb, None)

    @pl.when((kb >= 1) & (kb == qi))
    def _():
        attend(kx, vx, kb, 2)

    @pl.when(kb == qi + 1)
    def _():
        attend(kx, vx, kb, 1)
        for h in range(H):
            yb_ref[0, :, h * dh:(h + 1) * dh] = (acc_ref[h] / l_ref[h]).astype(yb_ref.dtype)


def _dsa(z3, zi3, k_meta, v_meta, ikt, rel_bias, *, tq, topk):
    nb, seq, _ = z3.shape
    H, dh = DSA_HEADS, DSA_DH
    nq = seq // tq
    n_chunks = nq + 1
    hd = H * dh
    q_blk = ZG_W // hd
    kmap = lambda b, i, j: (b, jnp.minimum(jnp.maximum(j - 1, 0), i), q_blk + 1)
    vmap = lambda b, i, j: (b, jnp.minimum(jnp.maximum(j - 1, 0), i), q_blk + 2)
    return pl.pallas_call(
        functools.partial(_dsa_kernel, tq=tq, topk=topk, sel_rows=min(tq, 128)),
        grid=(nb, nq, n_chunks),
        in_specs=[pl.BlockSpec((1, tq, hd), lambda b, i, j: (b, i, q_blk)),
                  pl.BlockSpec((1, tq, hd), kmap),
                  pl.BlockSpec((1, tq, hd), vmap),
                  pl.BlockSpec((tq, hd), lambda b, i, j: (0, 0)),
                  pl.BlockSpec((tq, hd), lambda b, i, j: (0, 0)),
                  pl.BlockSpec((1, tq, IQ_W + SMALL_W), lambda b, i, j: (b, i, 0)),
                  pl.BlockSpec((1, n_chunks, IDX_DH, tq), lambda b, i, j: (b, 0, 0, 0)),
                  pl.BlockSpec(memory_space=pltpu.SMEM)],
        out_specs=pl.BlockSpec((1, tq, hd), lambda b, i, j: (b, i, 0)),
        out_shape=jax.ShapeDtypeStruct((nb, seq, hd), BF16),
        scratch_shapes=[pltpu.VMEM((n_chunks, tq, tq), I32),
                        pltpu.VMEM((tq, 1), I32),
                        pltpu.VMEM((tq, hd), BF16),
                        pltpu.VMEM((H, tq, 1), F32),
                        pltpu.VMEM((H, tq, 1), F32),
                        pltpu.VMEM((H, tq, dh), F32),
                        pltpu.VMEM((3, H, tq, tq), F32)],
        compiler_params=_params("arbitrary", "arbitrary", "arbitrary"),
        name="dsa",
    )(z3, z3, z3, k_meta, v_meta, zi3, ikt, rel_bias)


def _merge_kernel(h_ref, ya_ref, yb_ref, g_ref, wa_ref, wb_ref, wo_ref, o_ref):
    d = h_ref.shape[1]
    g = g_ref[...].astype(F32)
    ua = jnp.dot(ya_ref[...], wa_ref[...], preferred_element_type=F32)
    ub = jnp.dot(yb_ref[...], wb_ref[...], preferred_element_type=F32)
    merged = _sigmoid(g[:, :d]) * ua + _sigmoid(g[:, d:]) * ub
    o_ref[...] = h_ref[...] + jnp.dot(merged.astype(BF16), wo_ref[...], preferred_element_type=F32)


def _merge(h, ya, yb, z, wa, wb, wo, *, tm):
    n, d = h.shape
    gate_blk = (ZG_W + ZD_W) // (2 * d)
    row = pl.BlockSpec((tm, d), lambda i: (i, 0))
    wspec = pl.BlockSpec((d, d), lambda i: (0, 0))
    return pl.pallas_call(
        _merge_kernel,
        grid=(n // tm,),
        in_specs=[row, row, row, pl.BlockSpec((tm, 2 * d), lambda i: (i, gate_blk)), wspec, wspec, wspec],
        out_specs=row,
        out_shape=jax.ShapeDtypeStruct((n, d), F32),
        compiler_params=_params("parallel"),
        name="merge",
    )(h, ya, yb, z, wa, wb, wo)


def kernel(x, meta_tokens, norm_ffn1, ffn1_w_gate, ffn1_w_up, ffn1_w_down, norm_mix, w_in, gla_w_lr, gla_b_lr,
           gla_norm, w_up_a, w_up_b, w_out, rel_bias, norm_ffn2, ffn2_w_gate, ffn2_w_up, ffn2_w_down, norm_final):
    nb, seq, d = x.shape
    assert norm_ffn1.shape[0] == 1, "single-layer block only"
    assert meta_tokens.shape[0] == N_META
    tq = 256
    assert seq % tq == 0 and (ZG_W + ZD_W) % (2 * d) == 0 and ZG_W % (DSA_HEADS * DSA_DH) == 0
    topk = min(TOPK_MAX, seq // 4)
    dff = ffn1_w_gate.shape[2]
    tf = dff // 2
    bf = lambda a: a.astype(BF16)

    sizes = (GLA_HEADS * GLA_DK, GLA_HEADS * GLA_DK, GLA_HEADS * GLA_DV, GLA_HEADS * GLA_DV, GLA_LOWRANK,
             DSA_HEADS * DSA_DH, DSA_HEADS * DSA_DH, DSA_HEADS * DSA_DH, IQ_W, IDX_DH, IDX_HEADS, d, d)
    offs = np.concatenate([[0], np.cumsum(sizes)])
    (c_gq, c_gk, c_gv, c_gr, c_lr, c_dq, c_dk, c_dv, c_iq, c_ik, c_iw, c_ga, c_gb) = [
        w_in[0][:, offs[i]:offs[i + 1]] for i in range(len(sizes))]
    w_big = bf(jnp.concatenate([c_gq, c_gk, c_gv, c_gr, c_dq, c_dk, c_dv, c_ga, c_gb], axis=1))
    pad_w = SMALL_W - (IDX_DH + IDX_HEADS + GLA_LOWRANK)
    w_small = bf(jnp.concatenate([c_iq, c_ik, c_iw, c_lr, jnp.zeros((d, pad_w), F32)], axis=1))
    wlr_pad = jnp.zeros((SMALL_W, GLA_HEADS * GLA_DK), F32).at[LR_OFF:LR_OFF + GLA_LOWRANK].set(gla_w_lr[0])

    xr = x.reshape(nb * seq, d)
    g1, gm = norm_ffn1, norm_mix
    wg1, wu1, wd1 = bf(ffn1_w_gate[0]), bf(ffn1_w_up[0]), bf(ffn1_w_down[0])

    h1 = _ffn(xr, g1, wg1, wu1, wd1, tm=512, tf=tf)
    h1m = _ffn(meta_tokens, g1, wg1, wu1, wd1, tm=512, tf=tf)
    z = _proj(h1, gm, w_big, BF16, tm=512, tn=2048, name="proj_big")
    zi = _proj(h1, gm, w_small, F32, tm=512, tn=IQ_W + SMALL_W, name="proj_small")
    zm = _proj(h1m, gm, w_big, BF16, tm=512, tn=2048, name="proj_big_meta")
    zim = _proj(h1m, gm, w_small, F32, tm=512, tn=IQ_W + SMALL_W, name="proj_small_meta")

    z3 = z.reshape(nb, seq, z.shape[1])
    zi3 = zi.reshape(nb, seq, zi.shape[1])

    ya = _gla(z3, zi3, zm, zim, wlr_pad, gla_b_lr, gla_norm, chunk=128)

    hd = DSA_HEADS * DSA_DH
    k_meta = jnp.pad(zm[:, ZG_W + hd:ZG_W + 2 * hd], ((0, tq - N_META), (0, 0)))
    v_meta = jnp.pad(zm[:, ZG_W + 2 * hd:ZG_W + 3 * hd], ((0, tq - N_META), (0, 0)))
    ik_x = zi3[:, :, IQ_W + IK_OFF:IQ_W + IK_OFF + IDX_DH]
    ik_m = jnp.pad(zim[:, IQ_W + IK_OFF:IQ_W + IK_OFF + IDX_DH], ((0, tq - N_META), (0, 0)))
    ik_all = jnp.concatenate([jnp.broadcast_to(ik_m[None], (nb, tq, IDX_DH)), ik_x], axis=1)
    ikt = bf(ik_all.reshape(nb, seq // tq + 1, tq, IDX_DH).transpose(0, 1, 3, 2))
    yb = _dsa(z3, zi3, k_meta, v_meta, ikt, rel_bias, tq=tq, topk=topk)

    h2 = _merge(h1, ya.reshape(nb * seq, -1), yb.reshape(nb * seq, -1), z,
                bf(w_up_a[0]), bf(w_up_b[0]), bf(w_out[0]), tm=512)
    out = _ffn(h2, norm_ffn2, bf(ffn2_w_gate[0]), bf(ffn2_w_up[0]), bf(ffn2_w_down[0]), norm_final[None, :],
               tm=512, tf=tf)
    return out.reshape(nb, seq, d)
```

```python
import functools

import numpy as np
import jax
import jax.numpy as jnp
from jax import lax
from jax.experimental import pallas as pl
from jax.experimental.pallas import tpu as pltpu

F32 = jnp.float32
BF16 = jnp.bfloat16
I32 = jnp.int32
HIGHEST = lax.Precision.HIGHEST

N_META = 16
EPS = 1e-6
GLA_HEADS, GLA_DK, GLA_DV, GLA_LOWRANK, GLA_TAU = 4, 128, 256, 16, 16.0
DSA_HEADS, DSA_DH, IDX_HEADS, IDX_DH, TOPK_MAX = 8, 128, 8, 64, 256
REL_BUCKETS, REL_MAX_DIST = 32, 128

LANES = 128
SUBLANES = 8
VMEM_LIMIT = 56 * 1024 * 1024
INT_MIN = -(2 ** 31)

ZG_W = 2 * GLA_HEADS * GLA_DK + 2 * GLA_HEADS * GLA_DV
ZD_W = 3 * DSA_HEADS * DSA_DH
SMALL_W = LANES
IQ_W = IDX_HEADS * IDX_DH
IK_OFF, IW_OFF, LR_OFF = 0, IDX_DH, IDX_DH + IDX_HEADS


def _rel_bucket_thresholds():
    n = np.arange(0, 4 * REL_MAX_DIST)
    max_exact = REL_BUCKETS // 2
    nf = np.maximum(n, max_exact).astype(np.float32)
    large = max_exact + (np.log(nf / np.float32(max_exact)) / np.float32(np.log(REL_MAX_DIST / max_exact))
                         * np.float32(REL_BUCKETS - max_exact)).astype(np.int32)
    large64 = max_exact + (np.log(np.maximum(n, max_exact) / max_exact) / np.log(REL_MAX_DIST / max_exact)
                           * (REL_BUCKETS - max_exact)).astype(np.int64)
    assert (large == large64).all()
    bucket = np.where(n < max_exact, n, np.minimum(large, REL_BUCKETS - 1))
    assert (np.diff(bucket) >= 0).all()
    return [int(np.argmax(bucket >= b)) for b in range(REL_BUCKETS)]


BUCKET_START = _rel_bucket_thresholds()


def _rms(x, g):
    return x * lax.rsqrt(jnp.mean(x * x, axis=-1, keepdims=True) + EPS) * g


def _sigmoid(x):
    return 1.0 / (1.0 + jnp.exp(-x))


def _dot_nt(a, b):
    return lax.dot_general(a, b, (((1,), (1,)), ((), ())), preferred_element_type=F32)


def _params(*sem):
    return pltpu.CompilerParams(dimension_semantics=sem, vmem_limit_bytes=VMEM_LIMIT)


def _ffn_kernel(*refs, final_norm):
    if final_norm:
        x_ref, g_ref, wg_ref, wu_ref, wd_ref, gf_ref, o_ref, xn_ref = refs
    else:
        x_ref, g_ref, wg_ref, wu_ref, wd_ref, o_ref, xn_ref = refs
    j = pl.program_id(1)

    @pl.when(j == 0)
    def _():
        xn_ref[...] = _rms(x_ref[...], g_ref[...]).astype(BF16)
        o_ref[...] = jnp.zeros_like(o_ref)

    xn = xn_ref[...]
    a = jnp.dot(xn, wg_ref[...], preferred_element_type=F32)
    b = jnp.dot(xn, wu_ref[...], preferred_element_type=F32)
    hmid = (a * _sigmoid(a) * b).astype(BF16)
    o_ref[...] += jnp.dot(hmid, wd_ref[...], preferred_element_type=F32)

    @pl.when(j == pl.num_programs(1) - 1)
    def _():
        h = x_ref[...] + 0.5 * o_ref[...]
        if final_norm:
            h = _rms(h, gf_ref[...])
        o_ref[...] = h


def _ffn(x, g, wg, wu, wd, gf=None, *, tm, tf):
    n, d = x.shape
    dff = wg.shape[1]
    tm = min(tm, n)
    row = pl.BlockSpec((tm, d), lambda i, j: (i, 0))
    vec = pl.BlockSpec((1, d), lambda i, j: (0, 0))
    in_specs = [row, vec,
                pl.BlockSpec((d, tf), lambda i, j: (0, j)),
                pl.BlockSpec((d, tf), lambda i, j: (0, j)),
                pl.BlockSpec((tf, d), lambda i, j: (j, 0))]
    args = [x, g, wg, wu, wd]
    if gf is not None:
        in_specs.append(vec)
        args.append(gf)
    return pl.pallas_call(
        functools.partial(_ffn_kernel, final_norm=gf is not None),
        grid=(n // tm, dff // tf),
        in_specs=in_specs,
        out_specs=row,
        out_shape=jax.ShapeDtypeStruct((n, d), F32),
        scratch_shapes=[pltpu.VMEM((tm, d), BF16)],
        compiler_params=_params("parallel", "arbitrary"),
        name="ffn_final" if gf is not None else "ffn",
    )(*args)


def _proj_kernel(x_ref, g_ref, w_ref, o_ref, xn_ref):
    @pl.when(pl.program_id(1) == 0)
    def _():
        xn_ref[...] = _rms(x_ref[...], g_ref[...]).astype(BF16)

    o_ref[...] = jnp.dot(xn_ref[...], w_ref[...], preferred_element_type=F32).astype(o_ref.dtype)


def _proj(x, g, w, out_dtype, *, tm, tn, name):
    n, d = x.shape
    width = w.shape[1]
    tm = min(tm, n)
    tn = min(tn, width)
    return pl.pallas_call(
        _proj_kernel,
        grid=(n // tm, width // tn),
        in_specs=[pl.BlockSpec((tm, d), lambda i, j: (i, 0)),
                  pl.BlockSpec((1, d), lambda i, j: (0, 0)),
                  pl.BlockSpec((d, tn), lambda i, j: (0, j))],
        out_specs=pl.BlockSpec((tm, tn), lambda i, j: (i, j)),
        out_shape=jax.ShapeDtypeStruct((n, width), out_dtype),
        scratch_shapes=[pltpu.VMEM((tm, d), BF16)],
        compiler_params=_params("parallel", "arbitrary"),
        name=name,
    )(x, g, w)


def _lower_tri(n):
    r = lax.broadcasted_iota(I32, (n, n), 0)
    c = lax.broadcasted_iota(I32, (n, n), 1)
    return c <= r


def _gla_kernel(zg_ref, zs_ref, zgm_ref, zsm_ref, wlr_ref, blr_ref, gn_ref, ya_ref, st_ref, *, nb, chunk):
    H, dk, dv = GLA_HEADS, GLA_DK, GLA_DV
    kw = H * dk
    wlr = wlr_ref[...]
    blr = blr_ref[...]
    gn = gn_ref[...]

    def log_decay(zs):
        x = jnp.dot(zs, wlr, preferred_element_type=F32, precision=HIGHEST) + blr
        return (jnp.minimum(x, 0.0) - jnp.log(1.0 + jnp.exp(-jnp.abs(x)))) * (1.0 / GLA_TAU)

    def cumsum_rows(g):
        tri = _lower_tri(g.shape[0]).astype(F32)
        return jnp.dot(tri, g, preferred_element_type=F32, precision=HIGHEST)

    @pl.when(pl.program_id(0) == 0)
    def _():
        bc = cumsum_rows(log_decay(zsm_ref[...]))
        bl = bc[N_META - 1:N_META, :]
        zgm = zgm_ref[...]
        kl = (zgm[:, kw:2 * kw].astype(F32) * jnp.exp(bl - bc)).astype(BF16)
        v = zgm[:, 2 * kw:2 * kw + H * dv]
        for h in range(H):
            st = lax.dot_general(v[:, h * dv:(h + 1) * dv], kl[:, h * dk:(h + 1) * dk],
                                 (((0,), (0,)), ((), ())), preferred_element_type=F32)
            for b in range(nb):
                st_ref[b * H + h] = st

    causal = _lower_tri(chunk)
    mid = chunk // 2
    for b in range(nb):
        zg = zg_ref[b]
        bc = cumsum_rows(log_decay(zs_ref[b]))
        bl = bc[chunk - 1:chunk, :]
        bm = bc[mid - 1:mid, :]
        qe = zg[:, :kw].astype(F32) * jnp.exp(bc - bm)
        ke = zg[:, kw:2 * kw].astype(F32) * jnp.exp(bm - bc)
        qs = (qe * jnp.exp(bm)).astype(BF16)
        kl = (ke * jnp.exp(bl - bm)).astype(BF16)
        qe = qe.astype(BF16)
        ke = ke.astype(BF16)
        dec = jnp.exp(bl)
        for h in range(H):
            ks = slice(h * dk, (h + 1) * dk)
            v = zg[:, 2 * kw + h * dv:2 * kw + (h + 1) * dv]
            r = zg[:, 2 * kw + H * dv + h * dv:2 * kw + H * dv + (h + 1) * dv].astype(F32)
            a = jnp.where(causal, _dot_nt(qe[:, ks], ke[:, ks]), 0.0).astype(BF16)
            st = st_ref[b * H + h]
            o = jnp.dot(a, v, preferred_element_type=F32) + _dot_nt(qs[:, ks], st.astype(BF16))
            st_ref[b * H + h] = st * dec[:, ks] + lax.dot_general(
                v, kl[:, ks], (((0,), (0,)), ((), ())), preferred_element_type=F32)
            y = _rms(o, gn) * (r * _sigmoid(r))
            ya_ref[b, :, h * dv:(h + 1) * dv] = y.astype(ya_ref.dtype)


def _gla(z3, zi3, z_meta, zi_meta, wlr_pad, blr, gn, *, chunk):
    nb, seq, _ = z3.shape
    H, dk, dv = GLA_HEADS, GLA_DK, GLA_DV
    small_blk = IQ_W // SMALL_W
    return pl.pallas_call(
        functools.partial(_gla_kernel, nb=nb, chunk=chunk),
        grid=(seq // chunk,),
        in_specs=[pl.BlockSpec((nb, chunk, ZG_W), lambda c: (0, c, 0)),
                  pl.BlockSpec((nb, chunk, SMALL_W), lambda c: (0, c, small_blk)),
                  pl.BlockSpec((N_META, ZG_W), lambda c: (0, 0)),
                  pl.BlockSpec((N_META, SMALL_W), lambda c: (0, small_blk)),
                  pl.BlockSpec((SMALL_W, H * dk), lambda c: (0, 0)),
                  pl.BlockSpec((1, H * dk), lambda c: (0, 0)),
                  pl.BlockSpec((1, dv), lambda c: (0, 0))],
        out_specs=pl.BlockSpec((nb, chunk, H * dv), lambda c: (0, c, 0)),
        out_shape=jax.ShapeDtypeStruct((nb, seq, H * dv), BF16),
        scratch_shapes=[pltpu.VMEM((nb * H, dv, dk), F32)],
        compiler_params=_params("arbitrary"),
        name="gla",
    )(z3, zi3, z_meta, zi_meta, wlr_pad, blr, gn)


def _sort_key(score):
    bits = pltpu.bitcast(score, I32)
    return bits ^ (lax.shift_right_arithmetic(bits, 31) & 0x7FFFFFFF)


def _dsa_kernel(bt_ref, qt_ref, kt_ref,
                q_ref, k_ref, vt_ref, km_ref, vmt_ref, zi_ref, ik_ref, relb_ref, yb_ref,
                skey_ref, thr_ref, neg_ref, m_ref, l_ref, acc_ref, bias_ref, *, tq, topk):
    H, dh = DSA_HEADS, DSA_DH
    step = pl.program_id(0)
    qi, kb = qt_ref[step], kt_ref[step]
    n_chunks = qi + 2
    krow = lax.broadcasted_iota(I32, (tq, tq), 0)
    qcol = lax.broadcasted_iota(I32, (tq, tq), 1)

    @pl.when(step == 0)
    def _():
        for t, dist in enumerate((N_META + qcol - krow, qcol - krow, tq + qcol - krow)):
            for h in range(H):
                far = relb_ref[REL_BUCKETS - 1, h]
                tile = jnp.full((tq, tq), relb_ref[0, h] - far, F32)
                for bkt in range(1, REL_BUCKETS):
                    tile = jnp.where(dist >= BUCKET_START[bkt], relb_ref[bkt, h] - far, tile)
                bias_ref[t, h] = tile

    @pl.when(kb == 0)
    def _():
        zi = zi_ref[0]
        m_ref[...] = jnp.full(m_ref.shape, -1e30, F32)
        l_ref[...] = jnp.zeros(l_ref.shape, F32)
        acc_ref[...] = jnp.zeros(acc_ref.shape, F32)

        iq = [zi[:, h * IDX_DH:(h + 1) * IDX_DH].astype(BF16) for h in range(IDX_HEADS)]
        small_t = zi[:, IQ_W:].T
        iw = [small_t[IW_OFF + h:IW_OFF + h + 1, :] for h in range(IDX_HEADS)]

        def score_keys(ci):
            ikc = ik_ref[0, ci]
            sc = jnp.zeros((tq, tq), F32)
            for h in range(IDX_HEADS):
                sc += iw[h] * jnp.maximum(_dot_nt(ikc, iq[h]), 0.0)
            return _sort_key(sc)

        skey_ref[0] = jnp.where(krow < N_META, score_keys(0), INT_MIN)

        def full_chunk(ci, carry):
            skey_ref[ci] = score_keys(ci)
            return carry

        lax.fori_loop(1, qi + 1, full_chunk, 0)
        skey_ref[qi + 1] = jnp.where(krow <= qcol, score_keys(qi + 1), INT_MIN)

        n_acc = 4

        def count_ge(cand):
            cand8 = jnp.broadcast_to(cand, (SUBLANES, tq))

            def body(ci, accs):
                accs = list(accs)
                for j in range(tq // SUBLANES):
                    blk = skey_ref[ci, j * SUBLANES:(j + 1) * SUBLANES, :]
                    accs[j % n_acc] = accs[j % n_acc] + jnp.where(blk >= cand8, 1, 0)
                return tuple(accs)

            accs = lax.fori_loop(0, n_chunks, body,
                                 tuple(jnp.zeros((SUBLANES, tq), I32) for _ in range(n_acc)))
            return jnp.sum(sum(accs[1:], accs[0]), axis=0, keepdims=True)

        def search(p, prefix):
            cand = prefix ^ lax.shift_left(jnp.int32(1), 31 - p)
            return jnp.where(count_ge(cand) >= topk, cand, prefix)

        prefix = lax.fori_loop(0, 32, search, jnp.full((1, tq), INT_MIN, I32))
        thr = jnp.maximum(prefix, INT_MIN + 1)
        thr_ref[...] = thr

        n_ge = count_ge(thr)
        n_gt = count_ge(thr + 1)

        @pl.when(jnp.max(n_ge) > topk)
        def _():
            need = (topk - n_gt).astype(F32)
            incl = (qcol <= krow).astype(BF16)

            def fix(ci, seen):
                sk = skey_ref[ci]
                tie = sk == thr
                tie_f = jnp.where(tie, 1.0, 0.0)
                run = jnp.dot(incl, tie_f.astype(BF16), preferred_element_type=F32)
                drop = tie & (seen + run > need)
                skey_ref[ci] = jnp.where(drop, INT_MIN, sk)
                return seen + jnp.sum(tie_f, axis=0, keepdims=True)

            lax.fori_loop(0, n_chunks, fix, jnp.zeros((1, tq), F32))

    def attend(kt_ref, vtt_ref, chunk, bias_idx):
        neg_ref[...] = jnp.where(skey_ref[chunk] >= thr_ref[...], 0.0, -jnp.inf)
        for h in range(H):
            hs = slice(h * dh, (h + 1) * dh)
            s = _dot_nt(kt_ref[:, hs], q_ref[0, :, hs]) + neg_ref[...]
            if bias_idx is not None:
                s += bias_ref[bias_idx, h]
            m_old = m_ref[h]
            m_new = jnp.maximum(m_old, jnp.max(s, axis=0, keepdims=True))
            alpha = jnp.exp(m_old - m_new)
            p = jnp.exp(s - m_new)
            l_ref[h] = alpha * l_ref[h] + jnp.sum(p, axis=0, keepdims=True)
            acc_ref[h] = alpha * acc_ref[h] + jnp.dot(vtt_ref[hs, :], p.astype(BF16),
                                                      preferred_element_type=F32)
            m_ref[h] = m_new

    kx = k_ref.at[0]
    vx = vt_ref.at[0]

    @pl.when((kb == 0) & (qi == 0))
    def _():
        attend(km_ref, vmt_ref, 0, 0)

    @pl.when((kb == 0) & (qi > 0))
    def _():
        attend(km_ref, vmt_ref, 0, None)

    @pl.when((kb >= 1) & (kb < qi))
    def _():
        attend(kx, vx, kb, None)

    @pl.when((kb >= 1) & (kb == qi))
    def _():
        attend(kx, vx, kb, 2)

    @pl.when(kb == qi + 1)
    def _():
        attend(kx, vx, kb, 1)
        for h in range(H):
            yb_ref[0, :, h * dh:(h + 1) * dh] = (acc_ref[h] / l_ref[h]).T.astype(yb_ref.dtype)


def _dsa(z3, vt, zi3, k_meta, vt_meta, ik, rel_bias, *, tq, topk):
    nb, seq, _ = z3.shape
    H, dh = DSA_HEADS, DSA_DH
    nq = seq // tq
    n_chunks = nq + 1
    hd = H * dh
    q_blk = ZG_W // hd
    sched = np.array([(b, i, j) for b in range(nb) for i in range(nq) for j in range(i + 2)], np.int32)
    bt, qt, kt = (jnp.asarray(sched[:, c]) for c in range(3))
    grid_spec = pltpu.PrefetchScalarGridSpec(
        num_scalar_prefetch=3,
        grid=(sched.shape[0],),
        in_specs=[pl.BlockSpec((1, tq, hd), lambda s, b, i, j: (b[s], i[s], q_blk)),
                  pl.BlockSpec((1, tq, hd), lambda s, b, i, j: (b[s], jnp.maximum(j[s] - 1, 0), q_blk + 1)),
                  pl.BlockSpec((1, hd, tq), lambda s, b, i, j: (b[s], 0, jnp.maximum(j[s] - 1, 0))),
                  pl.BlockSpec((tq, hd), lambda s, b, i, j: (0, 0)),
                  pl.BlockSpec((hd, tq), lambda s, b, i, j: (0, 0)),
                  pl.BlockSpec((1, tq, IQ_W + SMALL_W), lambda s, b, i, j: (b[s], i[s], 0)),
                  pl.BlockSpec((1, n_chunks, tq, IDX_DH), lambda s, b, i, j: (b[s], 0, 0, 0)),
                  pl.BlockSpec(memory_space=pltpu.SMEM)],
        out_specs=pl.BlockSpec((1, tq, hd), lambda s, b, i, j: (b[s], i[s], 0)),
        scratch_shapes=[pltpu.VMEM((n_chunks, tq, tq), I32),
                        pltpu.VMEM((1, tq), I32),
                        pltpu.VMEM((tq, tq), F32),
                        pltpu.VMEM((H, 1, tq), F32),
                        pltpu.VMEM((H, 1, tq), F32),
                        pltpu.VMEM((H, dh, tq), F32),
                        pltpu.VMEM((3, H, tq, tq), F32)])
    return pl.pallas_call(
        functools.partial(_dsa_kernel, tq=tq, topk=topk),
        grid_spec=grid_spec,
        out_shape=jax.ShapeDtypeStruct((nb, seq, hd), BF16),
        compiler_params=_params("arbitrary"),
        name="dsa",
    )(bt, qt, kt, z3, z3, vt, k_meta, vt_meta, zi3, ik, rel_bias)


def _merge_kernel(h_ref, ya_ref, yb_ref, g_ref, wa_ref, wb_ref, wo_ref, o_ref):
    d = h_ref.shape[1]
    g = g_ref[...].astype(F32)
    ua = jnp.dot(ya_ref[...], wa_ref[...], preferred_element_type=F32)
    ub = jnp.dot(yb_ref[...], wb_ref[...], preferred_element_type=F32)
    merged = _sigmoid(g[:, :d]) * ua + _sigmoid(g[:, d:]) * ub
    o_ref[...] = h_ref[...] + jnp.dot(merged.astype(BF16), wo_ref[...], preferred_element_type=F32)


def _merge(h, ya, yb, z, wa, wb, wo, *, tm):
    n, d = h.shape
    gate_blk = (ZG_W + ZD_W) // (2 * d)
    row = pl.BlockSpec((tm, d), lambda i: (i, 0))
    wspec = pl.BlockSpec((d, d), lambda i: (0, 0))
    return pl.pallas_call(
        _merge_kernel,
        grid=(n // tm,),
        in_specs=[row, row, row, pl.BlockSpec((tm, 2 * d), lambda i: (i, gate_blk)), wspec, wspec, wspec],
        out_specs=row,
        out_shape=jax.ShapeDtypeStruct((n, d), F32),
        compiler_params=_params("parallel"),
        name="merge",
    )(h, ya, yb, z, wa, wb, wo)


def kernel(x, meta_tokens, norm_ffn1, ffn1_w_gate, ffn1_w_up, ffn1_w_down, norm_mix, w_in, gla_w_lr, gla_b_lr,
           gla_norm, w_up_a, w_up_b, w_out, rel_bias, norm_ffn2, ffn2_w_gate, ffn2_w_up, ffn2_w_down, norm_final):
    nb, seq, d = x.shape
    assert norm_ffn1.shape[0] == 1, "single-layer block only"
    assert meta_tokens.shape[0] == N_META
    tq = 256
    assert seq % tq == 0 and (ZG_W + ZD_W) % (2 * d) == 0 and ZG_W % (DSA_HEADS * DSA_DH) == 0
    topk = min(TOPK_MAX, seq // 4)
    dff = ffn1_w_gate.shape[2]
    tf = dff // 2
    bf = lambda a: a.astype(BF16)

    sizes = (GLA_HEADS * GLA_DK, GLA_HEADS * GLA_DK, GLA_HEADS * GLA_DV, GLA_HEADS * GLA_DV, GLA_LOWRANK,
             DSA_HEADS * DSA_DH, DSA_HEADS * DSA_DH, DSA_HEADS * DSA_DH, IQ_W, IDX_DH, IDX_HEADS, d, d)
    offs = np.concatenate([[0], np.cumsum(sizes)])
    (c_gq, c_gk, c_gv, c_gr, c_lr, c_dq, c_dk, c_dv, c_iq, c_ik, c_iw, c_ga, c_gb) = [
        w_in[0][:, offs[i]:offs[i + 1]] for i in range(len(sizes))]
    c_gq = c_gq * (GLA_DK ** -0.5)
    c_dq = c_dq * (DSA_DH ** -0.5)
    c_iw = c_iw * ((IDX_HEADS ** -0.5) * (IDX_DH ** -0.5))
    w_big = bf(jnp.concatenate([c_gq, c_gk, c_gv, c_gr, c_dq, c_dk, c_dv, c_ga, c_gb], axis=1))
    pad_w = SMALL_W - (IDX_DH + IDX_HEADS + GLA_LOWRANK)
    w_small = bf(jnp.concatenate([c_iq, c_ik, c_iw, c_lr, jnp.zeros((d, pad_w), F32)], axis=1))
    wlr_pad = jnp.zeros((SMALL_W, GLA_HEADS * GLA_DK), F32).at[LR_OFF:LR_OFF + GLA_LOWRANK].set(gla_w_lr[0])

    xr = x.reshape(nb * seq, d)
    g1, gm = norm_ffn1, norm_mix
    wg1, wu1, wd1 = bf(ffn1_w_gate[0]), bf(ffn1_w_up[0]), bf(ffn1_w_down[0])

    h1 = _ffn(xr, g1, wg1, wu1, wd1, tm=512, tf=tf)
    h1m = _ffn(meta_tokens, g1, wg1, wu1, wd1, tm=512, tf=tf)
    z = _proj(h1, gm, w_big, BF16, tm=512, tn=2048, name="proj_big")
    zi = _proj(h1, gm, w_small, F32, tm=512, tn=IQ_W + SMALL_W, name="proj_small")
    zm = _proj(h1m, gm, w_big, BF16, tm=512, tn=2048, name="proj_big_meta")
    zim = _proj(h1m, gm, w_small, F32, tm=512, tn=IQ_W + SMALL_W, name="proj_small_meta")

    z3 = z.reshape(nb, seq, z.shape[1])
    zi3 = zi.reshape(nb, seq, zi.shape[1])

    ya = _gla(z3, zi3, zm, zim, wlr_pad, gla_b_lr, gla_norm, chunk=128)

    hd = DSA_HEADS * DSA_DH
    vt = jnp.swapaxes(z3[:, :, ZG_W + 2 * hd:ZG_W + 3 * hd], 1, 2)
    k_meta = jnp.pad(zm[:, ZG_W + hd:ZG_W + 2 * hd], ((0, tq - N_META), (0, 0)))
    vt_meta = jnp.pad(zm[:, ZG_W + 2 * hd:ZG_W + 3 * hd], ((0, tq - N_META), (0, 0))).T
    ik_x = zi3[:, :, IQ_W + IK_OFF:IQ_W + IK_OFF + IDX_DH]
    ik_m = jnp.pad(zim[:, IQ_W + IK_OFF:IQ_W + IK_OFF + IDX_DH], ((0, tq - N_META), (0, 0)))
    ik_all = jnp.concatenate([jnp.broadcast_to(ik_m[None], (nb, tq, IDX_DH)), ik_x], axis=1)
    ik = bf(ik_all.reshape(nb, seq // tq + 1, tq, IDX_DH))
    yb = _dsa(z3, vt, zi3, k_meta, vt_meta, ik, rel_bias, tq=tq, topk=topk)

    h2 = _merge(h1, ya.reshape(nb * seq, -1), yb.reshape(nb * seq, -1), z,
                bf(w_up_a[0]), bf(w_up_b[0]), bf(w_out[0]), tm=512)
    out = _ffn(h2, norm_ffn2, bf(ffn2_w_gate[0]), bf(ffn2_w_up[0]), bf(ffn2_w_down[0]), norm_final[None, :],
               tm=512, tf=tf)
    return out.reshape(nb, seq, d)
```

```python
import functools

import numpy as np
import jax
import jax.numpy as jnp
from jax import lax
from jax.experimental import pallas as pl
from jax.experimental.pallas import tpu as pltpu

F32 = jnp.float32
BF16 = jnp.bfloat16
I32 = jnp.int32
I16 = jnp.int16
ONE16 = np.int16(1)
ZERO16 = np.int16(0)
HALF16 = 2 ** 15
HIGHEST = lax.Precision.HIGHEST

N_META = 16
EPS = 1e-6
GLA_HEADS, GLA_DK, GLA_DV, GLA_LOWRANK, GLA_TAU = 4, 128, 256, 16, 16.0
DSA_HEADS, DSA_DH, IDX_HEADS, IDX_DH, TOPK_MAX = 8, 128, 8, 64, 256
REL_BUCKETS, REL_MAX_DIST = 32, 128

LANES = 128
SUBLANES = 8
VMEM_LIMIT = 56 * 1024 * 1024
INT_MIN = -(2 ** 31)
LOG2E = float(np.log2(np.e))
BF16_SUBLANES = 16
VROWS = DSA_DH + BF16_SUBLANES

ZG_W = 2 * GLA_HEADS * GLA_DK + 2 * GLA_HEADS * GLA_DV
ZD_W = 3 * DSA_HEADS * DSA_DH
SMALL_W = LANES
IQ_W = IDX_HEADS * IDX_DH
IK_OFF, IW_OFF, LR_OFF = 0, IDX_DH, IDX_DH + IDX_HEADS


def _rel_bucket_thresholds():
    n = np.arange(0, 4 * REL_MAX_DIST)
    max_exact = REL_BUCKETS // 2
    nf = np.maximum(n, max_exact).astype(np.float32)
    large = max_exact + (np.log(nf / np.float32(max_exact)) / np.float32(np.log(REL_MAX_DIST / max_exact))
                         * np.float32(REL_BUCKETS - max_exact)).astype(np.int32)
    large64 = max_exact + (np.log(np.maximum(n, max_exact) / max_exact) / np.log(REL_MAX_DIST / max_exact)
                           * (REL_BUCKETS - max_exact)).astype(np.int64)
    assert (large == large64).all()
    bucket = np.where(n < max_exact, n, np.minimum(large, REL_BUCKETS - 1))
    assert (np.diff(bucket) >= 0).all()
    return [int(np.argmax(bucket >= b)) for b in range(REL_BUCKETS)]


BUCKET_START = _rel_bucket_thresholds()


def _rms(x, g):
    return x * lax.rsqrt(jnp.mean(x * x, axis=-1, keepdims=True) + EPS) * g


def _sigmoid(x):
    return 1.0 / (1.0 + jnp.exp(-x))


def _dot_nt(a, b):
    return lax.dot_general(a, b, (((1,), (1,)), ((), ())), preferred_element_type=F32)


def _params(*sem):
    return pltpu.CompilerParams(dimension_semantics=sem, vmem_limit_bytes=VMEM_LIMIT)


def _ffn_kernel(*refs, final_norm):
    if final_norm:
        x_ref, g_ref, wg_ref, wu_ref, wd_ref, gf_ref, o_ref, xn_ref = refs
    else:
        x_ref, g_ref, wg_ref, wu_ref, wd_ref, o_ref, xn_ref = refs
    j = pl.program_id(1)

    @pl.when(j == 0)
    def _():
        xn_ref[...] = _rms(x_ref[...], g_ref[...]).astype(BF16)
        o_ref[...] = jnp.zeros_like(o_ref)

    xn = xn_ref[...]
    a = jnp.dot(xn, wg_ref[...], preferred_element_type=F32)
    b = jnp.dot(xn, wu_ref[...], preferred_element_type=F32)
    hmid = (a * _sigmoid(a) * b).astype(BF16)
    o_ref[...] += jnp.dot(hmid, wd_ref[...], preferred_element_type=F32)

    @pl.when(j == pl.num_programs(1) - 1)
    def _():
        h = x_ref[...] + 0.5 * o_ref[...]
        if final_norm:
            h = _rms(h, gf_ref[...])
        o_ref[...] = h


def _ffn(x, g, wg, wu, wd, gf=None, *, tm, tf):
    n, d = x.shape
    dff = wg.shape[1]
    tm = min(tm, n)
    row = pl.BlockSpec((tm, d), lambda i, j: (i, 0))
    vec = pl.BlockSpec((1, d), lambda i, j: (0, 0))
    in_specs = [row, vec,
                pl.BlockSpec((d, tf), lambda i, j: (0, j)),
                pl.BlockSpec((d, tf), lambda i, j: (0, j)),
                pl.BlockSpec((tf, d), lambda i, j: (j, 0))]
    args = [x, g, wg, wu, wd]
    if gf is not None:
        in_specs.append(vec)
        args.append(gf)
    return pl.pallas_call(
        functools.partial(_ffn_kernel, final_norm=gf is not None),
        grid=(n // tm, dff // tf),
        in_specs=in_specs,
        out_specs=row,
        out_shape=jax.ShapeDtypeStruct((n, d), F32),
        scratch_shapes=[pltpu.VMEM((tm, d), BF16)],
        compiler_params=_params("parallel", "arbitrary"),
        name="ffn_final" if gf is not None else "ffn",
    )(*args)


def _proj_kernel(x_ref, g_ref, w_ref, o_ref, xn_ref):
    @pl.when(pl.program_id(1) == 0)
    def _():
        xn_ref[...] = _rms(x_ref[...], g_ref[...]).astype(BF16)

    o_ref[...] = jnp.dot(xn_ref[...], w_ref[...], preferred_element_type=F32).astype(o_ref.dtype)


def _proj(x, g, w, out_dtype, *, tm, tn, name):
    n, d = x.shape
    width = w.shape[1]
    tm = min(tm, n)
    tn = min(tn, width)
    return pl.pallas_call(
        _proj_kernel,
        grid=(n // tm, width // tn),
        in_specs=[pl.BlockSpec((tm, d), lambda i, j: (i, 0)),
                  pl.BlockSpec((1, d), lambda i, j: (0, 0)),
                  pl.BlockSpec((d, tn), lambda i, j: (0, j))],
        out_specs=pl.BlockSpec((tm, tn), lambda i, j: (i, j)),
        out_shape=jax.ShapeDtypeStruct((n, width), out_dtype),
        scratch_shapes=[pltpu.VMEM((tm, d), BF16)],
        compiler_params=_params("parallel", "arbitrary"),
        name=name,
    )(x, g, w)


def _lower_tri(n):
    r = lax.broadcasted_iota(I32, (n, n), 0)
    c = lax.broadcasted_iota(I32, (n, n), 1)
    return c <= r


def _gla_kernel(zg_ref, zs_ref, zgm_ref, zsm_ref, wlr_ref, blr_ref, gn_ref, ya_ref, st_ref, *, nb, chunk):
    H, dk, dv = GLA_HEADS, GLA_DK, GLA_DV
    kw = H * dk
    wlr = wlr_ref[...]
    blr = blr_ref[...]
    gn = gn_ref[...]

    def log_decay(zs):
        x = jnp.dot(zs, wlr, preferred_element_type=F32, precision=HIGHEST) + blr
        return (jnp.minimum(x, 0.0) - jnp.log(1.0 + jnp.exp(-jnp.abs(x)))) * (1.0 / GLA_TAU)

    def cumsum_rows(g):
        tri = _lower_tri(g.shape[0]).astype(F32)
        return jnp.dot(tri, g, preferred_element_type=F32, precision=HIGHEST)

    @pl.when(pl.program_id(0) == 0)
    def _():
        bc = cumsum_rows(log_decay(zsm_ref[...]))
        bl = bc[N_META - 1:N_META, :]
        zgm = zgm_ref[...]
        kl = (zgm[:, kw:2 * kw].astype(F32) * jnp.exp(bl - bc)).astype(BF16)
        v = zgm[:, 2 * kw:2 * kw + H * dv]
        for h in range(H):
            st = lax.dot_general(v[:, h * dv:(h + 1) * dv], kl[:, h * dk:(h + 1) * dk],
                                 (((0,), (0,)), ((), ())), preferred_element_type=F32)
            for b in range(nb):
                st_ref[b * H + h] = st

    causal = _lower_tri(chunk)
    mid = chunk // 2
    for b in range(nb):
        zg = zg_ref[b]
        bc = cumsum_rows(log_decay(zs_ref[b]))
        bl = bc[chunk - 1:chunk, :]
        bm = bc[mid - 1:mid, :]
        qe = zg[:, :kw].astype(F32) * jnp.exp(bc - bm)
        ke = zg[:, kw:2 * kw].astype(F32) * jnp.exp(bm - bc)
        qs = (qe * jnp.exp(bm)).astype(BF16)
        kl = (ke * jnp.exp(bl - bm)).astype(BF16)
        qe = qe.astype(BF16)
        ke = ke.astype(BF16)
        dec = jnp.exp(bl)
        for h in range(H):
            ks = slice(h * dk, (h + 1) * dk)
            v = zg[:, 2 * kw + h * dv:2 * kw + (h + 1) * dv]
            r = zg[:, 2 * kw + H * dv + h * dv:2 * kw + H * dv + (h + 1) * dv].astype(F32)
            a = jnp.where(causal, _dot_nt(qe[:, ks], ke[:, ks]), 0.0).astype(BF16)
            st = st_ref[b * H + h]
            o = jnp.dot(a, v, preferred_element_type=F32) + _dot_nt(qs[:, ks], st.astype(BF16))
            st_ref[b * H + h] = st * dec[:, ks] + lax.dot_general(
                v, kl[:, ks], (((0,), (0,)), ((), ())), preferred_element_type=F32)
            y = _rms(o, gn) * (r * _sigmoid(r))
            ya_ref[b, :, h * dv:(h + 1) * dv] = y.astype(ya_ref.dtype)


def _gla(z3, zi3, z_meta, zi_meta, wlr_pad, blr, gn, *, chunk):
    nb, seq, _ = z3.shape
    H, dk, dv = GLA_HEADS, GLA_DK, GLA_DV
    small_blk = IQ_W // SMALL_W
    return pl.pallas_call(
        functools.partial(_gla_kernel, nb=nb, chunk=chunk),
        grid=(seq // chunk,),
        in_specs=[pl.BlockSpec((nb, chunk, ZG_W), lambda c: (0, c, 0)),
                  pl.BlockSpec((nb, chunk, SMALL_W), lambda c: (0, c, small_blk)),
                  pl.BlockSpec((N_META, ZG_W), lambda c: (0, 0)),
                  pl.BlockSpec((N_META, SMALL_W), lambda c: (0, small_blk)),
                  pl.BlockSpec((SMALL_W, H * dk), lambda c: (0, 0)),
                  pl.BlockSpec((1, H * dk), lambda c: (0, 0)),
                  pl.BlockSpec((1, dv), lambda c: (0, 0))],
        out_specs=pl.BlockSpec((nb, chunk, H * dv), lambda c: (0, c, 0)),
        out_shape=jax.ShapeDtypeStruct((nb, seq, H * dv), BF16),
        scratch_shapes=[pltpu.VMEM((nb * H, dv, dk), F32)],
        compiler_params=_params("arbitrary"),
        name="gla",
    )(z3, zi3, z_meta, zi_meta, wlr_pad, blr, gn)


def _sort_key(score):
    bits = pltpu.bitcast(score, I32)
    return bits ^ (lax.shift_right_arithmetic(bits, 31) & 0x7FFFFFFF)


def _dsa_kernel(bt_ref, qt_ref, kt_ref,
                q_ref, k_ref, vt_ref, km_ref, vmt_ref, zi_ref, ik_ref, relb_ref, yb_ref,
                skey_ref, hi_ref, lo_ref, thr_ref, neg_ref, s0_ref, s1_ref, s2_ref, m_ref, acc_ref, bias_ref,
                *, tq, topk):
    H, dh = DSA_HEADS, DSA_DH
    s_refs = (s0_ref, s1_ref, s2_ref)
    step = pl.program_id(0)
    qi, kb = qt_ref[step], kt_ref[step]
    n_chunks = qi + 2
    krow = lax.broadcasted_iota(I32, (tq, tq), 0)
    qcol = lax.broadcasted_iota(I32, (tq, tq), 1)

    @pl.when(step == 0)
    def _():
        for t, dist in enumerate((N_META + qcol - krow, qcol - krow, tq + qcol - krow)):
            for h in range(H):
                far = relb_ref[REL_BUCKETS - 1, h]
                tile = jnp.full((tq, tq), (relb_ref[0, h] - far) * LOG2E, F32)
                for bkt in range(1, REL_BUCKETS):
                    tile = jnp.where(dist >= BUCKET_START[bkt], (relb_ref[bkt, h] - far) * LOG2E, tile)
                bias_ref[t, h] = tile

    @pl.when(kb == 0)
    def _():
        zi = zi_ref[0]
        m_ref[...] = jnp.full(m_ref.shape, -1e30, F32)
        acc_ref[...] = jnp.zeros(acc_ref.shape, F32)

        iq = [zi[:, h * IDX_DH:(h + 1) * IDX_DH].astype(BF16) for h in range(IDX_HEADS)]
        small_t = zi[:, IQ_W:].T
        iw = [small_t[IW_OFF + h:IW_OFF + h + 1, :] for h in range(IDX_HEADS)]

        def score_keys(ci):
            ikc = ik_ref[0, ci]
            sc = jnp.zeros((tq, tq), F32)
            for h in range(IDX_HEADS):
                sc += iw[h] * jnp.maximum(_dot_nt(ikc, iq[h]), 0.0)
            return _sort_key(sc)

        def store_keys(ci, key):
            skey_ref[ci] = key
            hi_ref[ci] = lax.shift_right_arithmetic(key, 16).astype(I16)

        store_keys(0, jnp.where(krow < N_META, score_keys(0), INT_MIN))

        def full_chunk(ci, carry):
            store_keys(ci, score_keys(ci))
            return carry

        lax.fori_loop(1, qi + 1, full_chunk, 0)
        store_keys(qi + 1, jnp.where(krow <= qcol, score_keys(qi + 1), INT_MIN))

        n_acc = 4

        def count_ge(cand):
            cand8 = jnp.broadcast_to(cand, (SUBLANES, tq))

            def body(ci, accs):
                accs = list(accs)
                for j in range(tq // SUBLANES):
                    blk = skey_ref[ci, j * SUBLANES:(j + 1) * SUBLANES, :]
                    accs[j % n_acc] = accs[j % n_acc] + jnp.where(blk >= cand8, 1, 0)
                return tuple(accs)

            accs = lax.fori_loop(0, n_chunks, body,
                                 tuple(jnp.zeros((SUBLANES, tq), I32) for _ in range(n_acc)))
            return jnp.sum(sum(accs[1:], accs[0]), axis=0, keepdims=True)

        def count_ge16(ref, cand):
            cand16 = jnp.broadcast_to(cand, (BF16_SUBLANES, tq)).astype(I16)

            def body(ci, accs):
                accs = list(accs)
                for j in range(tq // BF16_SUBLANES):
                    blk = ref[ci, j * BF16_SUBLANES:(j + 1) * BF16_SUBLANES, :]
                    accs[j % n_acc] = accs[j % n_acc] + jnp.where(blk >= cand16, ONE16, ZERO16)
                return tuple(accs)

            accs = lax.fori_loop(0, n_chunks, body,
                                 tuple(jnp.zeros((BF16_SUBLANES, tq), I16) for _ in range(n_acc)))
            tot = sum((a.astype(I32) for a in accs[1:]), accs[0].astype(I32))
            return jnp.sum(tot, axis=0, keepdims=True)

        def search16(ref):
            def step(p, u):
                cand = u | lax.shift_left(jnp.int32(1), 15 - p)
                return jnp.where(count_ge16(ref, cand - HALF16) >= topk, cand, u)
            return lax.fori_loop(0, 16, step, jnp.zeros((1, tq), I32))

        hi_thr = search16(hi_ref) - HALF16

        def low_halves(ci, carry):
            key = skey_ref[ci]
            hi = lax.shift_right_arithmetic(key, 16)
            lo = (key & 0xFFFF) - HALF16
            lo = jnp.where(hi == hi_thr, lo, jnp.where(hi > hi_thr, HALF16 - 1, -HALF16))
            lo_ref[ci] = lo.astype(I16)
            return carry

        lax.fori_loop(0, n_chunks, low_halves, 0)
        prefix = hi_thr * (2 * HALF16) + search16(lo_ref)
        thr = jnp.maximum(prefix, INT_MIN + 1)
        thr_ref[...] = thr

        n_ge = count_ge(thr)
        n_gt = count_ge(thr + 1)

        @pl.when(jnp.max(n_ge) > topk)
        def _():
            need = (topk - n_gt).astype(F32)
            incl = (qcol <= krow).astype(BF16)

            def fix(ci, seen):
                sk = skey_ref[ci]
                tie = sk == thr
                tie_f = jnp.where(tie, 1.0, 0.0)
                run = jnp.dot(incl, tie_f.astype(BF16), preferred_element_type=F32)
                drop = tie & (seen + run > need)
                skey_ref[ci] = jnp.where(drop, INT_MIN, sk)
                return seen + jnp.sum(tie_f, axis=0, keepdims=True)

            lax.fori_loop(0, n_chunks, fix, jnp.zeros((1, tq), F32))

    def attend(kt_ref, vtt_ref, chunk, bias_idx):
        neg_ref[...] = jnp.where(skey_ref[chunk] >= thr_ref[...], 0.0, -jnp.inf)

        def logits(h):
            hs = slice(h * dh, (h + 1) * dh)
            s = _dot_nt(kt_ref[:, hs], q_ref[0, :, hs]) + neg_ref[...]
            if bias_idx is not None:
                s += bias_ref[bias_idx, h]
            s_refs[h % len(s_refs)][...] = s
            return jnp.max(s, axis=0, keepdims=True)

        def accumulate(h, cmax):
            m_old = m_ref[h]
            m_new = jnp.maximum(m_old, cmax)
            p = jnp.exp2(s_refs[h % len(s_refs)][...] - m_new).astype(BF16)
            pv = jnp.dot(vtt_ref[h * VROWS:(h + 1) * VROWS, :], p, preferred_element_type=F32)
            acc_ref[h] = jnp.exp2(m_old - m_new) * acc_ref[h] + pv
            m_ref[h] = m_new

        ahead = len(s_refs) - 1
        cmax = [logits(h) for h in range(ahead)]
        for h in range(H):
            if h + ahead < H:
                cmax.append(logits(h + ahead))
            accumulate(h, cmax[h])

    kx = k_ref.at[0]
    vx = vt_ref.at[0]

    @pl.when((kb == 0) & (qi == 0))
    def _():
        attend(km_ref, vmt_ref, 0, 0)

    @pl.when((kb == 0) & (qi > 0))
    def _():
        attend(km_ref, vmt_ref, 0, None)

    @pl.when((kb >= 1) & (kb < qi))
    def _():
        attend(kx, vx, kb, None)

    @pl.when((kb >= 1) & (kb == qi))
    def _():
        attend(kx, vx, kb, 2)

    @pl.when(kb == qi + 1)
    def _():
        attend(kx, vx, kb, 1)
        for h in range(H):
            out_t = acc_ref[h, :dh, :] / acc_ref[h, dh:dh + 1, :]
            yb_ref[0, :, h * dh:(h + 1) * dh] = out_t.T.astype(yb_ref.dtype)


def _dsa(z3, vt, zi3, k_meta, vt_meta, ik, rel_bias, *, tq, topk):
    nb, seq, _ = z3.shape
    H, dh = DSA_HEADS, DSA_DH
    nq = seq // tq
    n_chunks = nq + 1
    hd = H * dh
    q_blk = ZG_W // hd
    sched = np.array([(b, i, j) for b in range(nb) for i in range(nq) for j in range(i + 2)], np.int32)
    bt, qt, kt = (jnp.asarray(sched[:, c]) for c in range(3))
    grid_spec = pltpu.PrefetchScalarGridSpec(
        num_scalar_prefetch=3,
        grid=(sched.shape[0],),
        in_specs=[pl.BlockSpec((1, tq, hd), lambda s, b, i, j: (b[s], i[s], q_blk)),
                  pl.BlockSpec((1, tq, hd), lambda s, b, i, j: (b[s], jnp.maximum(j[s] - 1, 0), q_blk + 1)),
                  pl.BlockSpec((1, H * VROWS, tq), lambda s, b, i, j: (b[s], 0, jnp.maximum(j[s] - 1, 0))),
                  pl.BlockSpec((tq, hd), lambda s, b, i, j: (0, 0)),
                  pl.BlockSpec((H * VROWS, tq), lambda s, b, i, j: (0, 0)),
                  pl.BlockSpec((1, tq, IQ_W + SMALL_W), lambda s, b, i, j: (b[s], i[s], 0)),
                  pl.BlockSpec((1, n_chunks, tq, IDX_DH), lambda s, b, i, j: (b[s], 0, 0, 0)),
                  pl.BlockSpec(memory_space=pltpu.SMEM)],
        out_specs=pl.BlockSpec((1, tq, hd), lambda s, b, i, j: (b[s], i[s], 0)),
        scratch_shapes=[pltpu.VMEM((n_chunks, tq, tq), I32),
                        pltpu.VMEM((n_chunks, tq, tq), I16),
                        pltpu.VMEM((n_chunks, tq, tq), I16),
                        pltpu.VMEM((1, tq), I32),
                        pltpu.VMEM((tq, tq), F32),
                        pltpu.VMEM((tq, tq), F32),
                        pltpu.VMEM((tq, tq), F32),
                        pltpu.VMEM((tq, tq), F32),
                        pltpu.VMEM((H, 1, tq), F32),
                        pltpu.VMEM((H, VROWS, tq), F32),
                        pltpu.VMEM((3, H, tq, tq), F32)])
    return pl.pallas_call(
        functools.partial(_dsa_kernel, tq=tq, topk=topk),
        grid_spec=grid_spec,
        out_shape=jax.ShapeDtypeStruct((nb, seq, hd), BF16),
        compiler_params=_params("arbitrary"),
        name="dsa",
    )(bt, qt, kt, z3, z3, vt, k_meta, vt_meta, zi3, ik, rel_bias)


def _merge_kernel(h_ref, ya_ref, yb_ref, g_ref, wa_ref, wb_ref, wo_ref, o_ref):
    d = h_ref.shape[1]
    g = g_ref[...].astype(F32)
    ua = jnp.dot(ya_ref[...], wa_ref[...], preferred_element_type=F32)
    ub = jnp.dot(yb_ref[...], wb_ref[...], preferred_element_type=F32)
    merged = _sigmoid(g[:, :d]) * ua + _sigmoid(g[:, d:]) * ub
    o_ref[...] = h_ref[...] + jnp.dot(merged.astype(BF16), wo_ref[...], preferred_element_type=F32)


def _merge(h, ya, yb, z, wa, wb, wo, *, tm):
    n, d = h.shape
    gate_blk = (ZG_W + ZD_W) // (2 * d)
    row = pl.BlockSpec((tm, d), lambda i: (i, 0))
    wspec = pl.BlockSpec((d, d), lambda i: (0, 0))
    return pl.pallas_call(
        _merge_kernel,
        grid=(n // tm,),
        in_specs=[row, row, row, pl.BlockSpec((tm, 2 * d), lambda i: (i, gate_blk)), wspec, wspec, wspec],
        out_specs=row,
        out_shape=jax.ShapeDtypeStruct((n, d), F32),
        compiler_params=_params("parallel"),
        name="merge",
    )(h, ya, yb, z, wa, wb, wo)


def kernel(x, meta_tokens, norm_ffn1, ffn1_w_gate, ffn1_w_up, ffn1_w_down, norm_mix, w_in, gla_w_lr, gla_b_lr,
           gla_norm, w_up_a, w_up_b, w_out, rel_bias, norm_ffn2, ffn2_w_gate, ffn2_w_up, ffn2_w_down, norm_final):
    nb, seq, d = x.shape
    assert norm_ffn1.shape[0] == 1, "single-layer block only"
    assert meta_tokens.shape[0] == N_META
    tq = 256
    assert seq % tq == 0 and (ZG_W + ZD_W) % (2 * d) == 0 and ZG_W % (DSA_HEADS * DSA_DH) == 0
    topk = min(TOPK_MAX, seq // 4)
    dff = ffn1_w_gate.shape[2]
    tf = dff // 2
    bf = lambda a: a.astype(BF16)

    sizes = (GLA_HEADS * GLA_DK, GLA_HEADS * GLA_DK, GLA_HEADS * GLA_DV, GLA_HEADS * GLA_DV, GLA_LOWRANK,
             DSA_HEADS * DSA_DH, DSA_HEADS * DSA_DH, DSA_HEADS * DSA_DH, IQ_W, IDX_DH, IDX_HEADS, d, d)
    offs = np.concatenate([[0], np.cumsum(sizes)])
    (c_gq, c_gk, c_gv, c_gr, c_lr, c_dq, c_dk, c_dv, c_iq, c_ik, c_iw, c_ga, c_gb) = [
        w_in[0][:, offs[i]:offs[i + 1]] for i in range(len(sizes))]
    c_gq = c_gq * (GLA_DK ** -0.5)
    c_dq = c_dq * (DSA_DH ** -0.5 * LOG2E)
    c_iw = c_iw * ((IDX_HEADS ** -0.5) * (IDX_DH ** -0.5))
    w_big = bf(jnp.concatenate([c_gq, c_gk, c_gv, c_gr, c_dq, c_dk, c_dv, c_ga, c_gb], axis=1))
    pad_w = SMALL_W - (IDX_DH + IDX_HEADS + GLA_LOWRANK)
    w_small = bf(jnp.concatenate([c_iq, c_ik, c_iw, c_lr, jnp.zeros((d, pad_w), F32)], axis=1))
    wlr_pad = jnp.zeros((SMALL_W, GLA_HEADS * GLA_DK), F32).at[LR_OFF:LR_OFF + GLA_LOWRANK].set(gla_w_lr[0])

    xr = x.reshape(nb * seq, d)
    g1, gm = norm_ffn1, norm_mix
    wg1, wu1, wd1 = bf(ffn1_w_gate[0]), bf(ffn1_w_up[0]), bf(ffn1_w_down[0])

    h1 = _ffn(xr, g1, wg1, wu1, wd1, tm=512, tf=tf)
    h1m = _ffn(meta_tokens, g1, wg1, wu1, wd1, tm=512, tf=tf)
    z = _proj(h1, gm, w_big, BF16, tm=512, tn=2048, name="proj_big")
    zi = _proj(h1, gm, w_small, F32, tm=512, tn=IQ_W + SMALL_W, name="proj_small")
    zm = _proj(h1m, gm, w_big, BF16, tm=512, tn=2048, name="proj_big_meta")
    zim = _proj(h1m, gm, w_small, F32, tm=512, tn=IQ_W + SMALL_W, name="proj_small_meta")

    z3 = z.reshape(nb, seq, z.shape[1])
    zi3 = zi.reshape(nb, seq, zi.shape[1])

    ya = _gla(z3, zi3, zm, zim, wlr_pad, gla_b_lr, gla_norm, chunk=128)

    hd = DSA_HEADS * DSA_DH
    def values_t(v):
        vt4 = jnp.swapaxes(v, -1, -2).reshape(v.shape[:-2] + (DSA_HEADS, DSA_DH, v.shape[-2]))
        ones = jnp.ones(vt4.shape[:-2] + (BF16_SUBLANES, v.shape[-2]), v.dtype)
        return jnp.concatenate([vt4, ones], axis=-2).reshape(v.shape[:-2] + (DSA_HEADS * VROWS, v.shape[-2]))

    vt = values_t(z3[:, :, ZG_W + 2 * hd:ZG_W + 3 * hd])
    k_meta = jnp.pad(zm[:, ZG_W + hd:ZG_W + 2 * hd], ((0, tq - N_META), (0, 0)))
    vt_meta = values_t(jnp.pad(zm[:, ZG_W + 2 * hd:ZG_W + 3 * hd], ((0, tq - N_META), (0, 0))))
    ik_x = zi3[:, :, IQ_W + IK_OFF:IQ_W + IK_OFF + IDX_DH]
    ik_m = jnp.pad(zim[:, IQ_W + IK_OFF:IQ_W + IK_OFF + IDX_DH], ((0, tq - N_META), (0, 0)))
    ik_all = jnp.concatenate([jnp.broadcast_to(ik_m[None], (nb, tq, IDX_DH)), ik_x], axis=1)
    ik = bf(ik_all.reshape(nb, seq // tq + 1, tq, IDX_DH))
    yb = _dsa(z3, vt, zi3, k_meta, vt_meta, ik, rel_bias, tq=tq, topk=topk)

    h2 = _merge(h1, ya.reshape(nb * seq, -1), yb.reshape(nb * seq, -1), z,
                bf(w_up_a[0]), bf(w_up_b[0]), bf(w_out[0]), tm=512)
    out = _ffn(h2, norm_ffn2, bf(ffn2_w_gate[0]), bf(ffn2_w_up[0]), bf(ffn2_w_down[0]), norm_final[None, :],
               tm=512, tf=tf)
    return out.reshape(nb, seq, d)
```

```python
import functools

import numpy as np
import jax
import jax.numpy as jnp
from jax import lax
from jax.experimental import pallas as pl
from jax.experimental.pallas import tpu as pltpu

F32 = jnp.float32
BF16 = jnp.bfloat16
I32 = jnp.int32
I16 = jnp.int16
ONE16 = np.int16(1)
ZERO16 = np.int16(0)
HALF16 = 2 ** 15
HIGHEST = lax.Precision.HIGHEST

N_META = 16
EPS = 1e-6
GLA_HEADS, GLA_DK, GLA_DV, GLA_LOWRANK, GLA_TAU = 4, 128, 256, 16, 16.0
DSA_HEADS, DSA_DH, IDX_HEADS, IDX_DH, TOPK_MAX = 8, 128, 8, 64, 256
REL_BUCKETS, REL_MAX_DIST = 32, 128

LANES = 128
SUBLANES = 8
VMEM_LIMIT = 56 * 1024 * 1024
INT_MIN = -(2 ** 31)
LOG2E = float(np.log2(np.e))
BF16_SUBLANES = 16
VROWS = DSA_DH + BF16_SUBLANES
DSA_LOGIT_BUFFERS = 8

ZG_W = 2 * GLA_HEADS * GLA_DK + 2 * GLA_HEADS * GLA_DV
ZD_W = 3 * DSA_HEADS * DSA_DH
SMALL_W = LANES
IQ_W = IDX_HEADS * IDX_DH
IK_OFF, IW_OFF, LR_OFF = 0, IDX_DH, IDX_DH + IDX_HEADS


def _rel_bucket_thresholds():
    n = np.arange(0, 4 * REL_MAX_DIST)
    max_exact = REL_BUCKETS // 2
    nf = np.maximum(n, max_exact).astype(np.float32)
    large = max_exact + (np.log(nf / np.float32(max_exact)) / np.float32(np.log(REL_MAX_DIST / max_exact))
                         * np.float32(REL_BUCKETS - max_exact)).astype(np.int32)
    large64 = max_exact + (np.log(np.maximum(n, max_exact) / max_exact) / np.log(REL_MAX_DIST / max_exact)
                           * (REL_BUCKETS - max_exact)).astype(np.int64)
    assert (large == large64).all()
    bucket = np.where(n < max_exact, n, np.minimum(large, REL_BUCKETS - 1))
    assert (np.diff(bucket) >= 0).all()
    return [int(np.argmax(bucket >= b)) for b in range(REL_BUCKETS)]


BUCKET_START = _rel_bucket_thresholds()


def _rms(x, g):
    return x * lax.rsqrt(jnp.mean(x * x, axis=-1, keepdims=True) + EPS) * g


def _sigmoid(x):
    return 1.0 / (1.0 + jnp.exp(-x))


def _dot_nt(a, b):
    return lax.dot_general(a, b, (((1,), (1,)), ((), ())), preferred_element_type=F32)


def _params(*sem):
    return pltpu.CompilerParams(dimension_semantics=sem, vmem_limit_bytes=VMEM_LIMIT)


def _ffn_kernel(*refs, final_norm):
    if final_norm:
        x_ref, g_ref, wg_ref, wu_ref, wd_ref, gf_ref, o_ref, xn_ref = refs
    else:
        x_ref, g_ref, wg_ref, wu_ref, wd_ref, o_ref, xn_ref = refs
    j = pl.program_id(1)

    @pl.when(j == 0)
    def _():
        xn_ref[...] = _rms(x_ref[...], g_ref[...]).astype(BF16)
        o_ref[...] = jnp.zeros_like(o_ref)

    xn = xn_ref[...]
    a = jnp.dot(xn, wg_ref[...], preferred_element_type=F32)
    b = jnp.dot(xn, wu_ref[...], preferred_element_type=F32)
    hmid = (a * _sigmoid(a) * b).astype(BF16)
    o_ref[...] += jnp.dot(hmid, wd_ref[...], preferred_element_type=F32)

    @pl.when(j == pl.num_programs(1) - 1)
    def _():
        h = x_ref[...] + 0.5 * o_ref[...]
        if final_norm:
            h = _rms(h, gf_ref[...])
        o_ref[...] = h


def _ffn(x, g, wg, wu, wd, gf=None, *, tm, tf):
    n, d = x.shape
    dff = wg.shape[1]
    tm = min(tm, n)
    row = pl.BlockSpec((tm, d), lambda i, j: (i, 0))
    vec = pl.BlockSpec((1, d), lambda i, j: (0, 0))
    in_specs = [row, vec,
                pl.BlockSpec((d, tf), lambda i, j: (0, j)),
                pl.BlockSpec((d, tf), lambda i, j: (0, j)),
                pl.BlockSpec((tf, d), lambda i, j: (j, 0))]
    args = [x, g, wg, wu, wd]
    if gf is not None:
        in_specs.append(vec)
        args.append(gf)
    return pl.pallas_call(
        functools.partial(_ffn_kernel, final_norm=gf is not None),
        grid=(n // tm, dff // tf),
        in_specs=in_specs,
        out_specs=row,
        out_shape=jax.ShapeDtypeStruct((n, d), F32),
        scratch_shapes=[pltpu.VMEM((tm, d), BF16)],
        compiler_params=_params("parallel", "arbitrary"),
        name="ffn_final" if gf is not None else "ffn",
    )(*args)


def _proj_kernel(x_ref, g_ref, w_ref, o_ref, xn_ref):
    @pl.when(pl.program_id(1) == 0)
    def _():
        xn_ref[...] = _rms(x_ref[...], g_ref[...]).astype(BF16)

    o_ref[...] = jnp.dot(xn_ref[...], w_ref[...], preferred_element_type=F32).astype(o_ref.dtype)


def _proj(x, g, w, out_dtype, *, tm, tn, name):
    n, d = x.shape
    width = w.shape[1]
    tm = min(tm, n)
    tn = min(tn, width)
    return pl.pallas_call(
        _proj_kernel,
        grid=(n // tm, width // tn),
        in_specs=[pl.BlockSpec((tm, d), lambda i, j: (i, 0)),
                  pl.BlockSpec((1, d), lambda i, j: (0, 0)),
                  pl.BlockSpec((d, tn), lambda i, j: (0, j))],
        out_specs=pl.BlockSpec((tm, tn), lambda i, j: (i, j)),
        out_shape=jax.ShapeDtypeStruct((n, width), out_dtype),
        scratch_shapes=[pltpu.VMEM((tm, d), BF16)],
        compiler_params=_params("parallel", "arbitrary"),
        name=name,
    )(x, g, w)


def _lower_tri(n):
    r = lax.broadcasted_iota(I32, (n, n), 0)
    c = lax.broadcasted_iota(I32, (n, n), 1)
    return c <= r


def _gla_kernel(zg_ref, zs_ref, zgm_ref, zsm_ref, wlr_ref, blr_ref, gn_ref, ya_ref, st_ref, *, nb, chunk):
    H, dk, dv = GLA_HEADS, GLA_DK, GLA_DV
    kw = H * dk
    wlr = wlr_ref[...]
    blr = blr_ref[...]
    gn = gn_ref[...]

    def log_decay(zs):
        x = jnp.dot(zs, wlr, preferred_element_type=F32, precision=HIGHEST) + blr
        return (jnp.minimum(x, 0.0) - jnp.log(1.0 + jnp.exp(-jnp.abs(x)))) * (1.0 / GLA_TAU)

    def cumsum_rows(g):
        tri = _lower_tri(g.shape[0]).astype(F32)
        return jnp.dot(tri, g, preferred_element_type=F32, precision=HIGHEST)

    @pl.when(pl.program_id(0) == 0)
    def _():
        bc = cumsum_rows(log_decay(zsm_ref[...]))
        bl = bc[N_META - 1:N_META, :]
        zgm = zgm_ref[...]
        kl = (zgm[:, kw:2 * kw].astype(F32) * jnp.exp(bl - bc)).astype(BF16)
        v = zgm[:, 2 * kw:2 * kw + H * dv]
        for h in range(H):
            st = lax.dot_general(v[:, h * dv:(h + 1) * dv], kl[:, h * dk:(h + 1) * dk],
                                 (((0,), (0,)), ((), ())), preferred_element_type=F32)
            for b in range(nb):
                st_ref[b * H + h] = st

    causal = _lower_tri(chunk)
    mid = chunk // 2
    for b in range(nb):
        zg = zg_ref[b]
        bc = cumsum_rows(log_decay(zs_ref[b]))
        bl = bc[chunk - 1:chunk, :]
        bm = bc[mid - 1:mid, :]
        qe = zg[:, :kw].astype(F32) * jnp.exp(bc - bm)
        ke = zg[:, kw:2 * kw].astype(F32) * jnp.exp(bm - bc)
        qs = (qe * jnp.exp(bm)).astype(BF16)
        kl = (ke * jnp.exp(bl - bm)).astype(BF16)
        qe = qe.astype(BF16)
        ke = ke.astype(BF16)
        dec = jnp.exp(bl)
        for h in range(H):
            ks = slice(h * dk, (h + 1) * dk)
            v = zg[:, 2 * kw + h * dv:2 * kw + (h + 1) * dv]
            r = zg[:, 2 * kw + H * dv + h * dv:2 * kw + H * dv + (h + 1) * dv].astype(F32)
            a = jnp.where(causal, _dot_nt(qe[:, ks], ke[:, ks]), 0.0).astype(BF16)
            st = st_ref[b * H + h]
            o = jnp.dot(a, v, preferred_element_type=F32) + _dot_nt(qs[:, ks], st.astype(BF16))
            st_ref[b * H + h] = st * dec[:, ks] + lax.dot_general(
                v, kl[:, ks], (((0,), (0,)), ((), ())), preferred_element_type=F32)
            y = _rms(o, gn) * (r * _sigmoid(r))
            ya_ref[b, :, h * dv:(h + 1) * dv] = y.astype(ya_ref.dtype)


def _gla(z3, zi3, z_meta, zi_meta, wlr_pad, blr, gn, *, chunk):
    nb, seq, _ = z3.shape
    H, dk, dv = GLA_HEADS, GLA_DK, GLA_DV
    small_blk = IQ_W // SMALL_W
    return pl.pallas_call(
        functools.partial(_gla_kernel, nb=nb, chunk=chunk),
        grid=(seq // chunk,),
        in_specs=[pl.BlockSpec((nb, chunk, ZG_W), lambda c: (0, c, 0)),
                  pl.BlockSpec((nb, chunk, SMALL_W), lambda c: (0, c, small_blk)),
                  pl.BlockSpec((N_META, ZG_W), lambda c: (0, 0)),
                  pl.BlockSpec((N_META, SMALL_W), lambda c: (0, small_blk)),
                  pl.BlockSpec((SMALL_W, H * dk), lambda c: (0, 0)),
                  pl.BlockSpec((1, H * dk), lambda c: (0, 0)),
                  pl.BlockSpec((1, dv), lambda c: (0, 0))],
        out_specs=pl.BlockSpec((nb, chunk, H * dv), lambda c: (0, c, 0)),
        out_shape=jax.ShapeDtypeStruct((nb, seq, H * dv), BF16),
        scratch_shapes=[pltpu.VMEM((nb * H, dv, dk), F32)],
        compiler_params=_params("arbitrary"),
        name="gla",
    )(z3, zi3, z_meta, zi_meta, wlr_pad, blr, gn)


def _sort_key(score):
    bits = pltpu.bitcast(score, I32)
    return bits ^ (lax.shift_right_arithmetic(bits, 31) & 0x7FFFFFFF)


def _dsa_kernel(q_ref, zi_ref, ik_ref, relb_ref, km_hbm, z_hbm, vt_hbm, yb_ref,
                skey_ref, hi_ref, lo_ref, thr_ref, neg_ref, s_ref, cm_ref, m_ref, acc_ref, bias_ref,
                kbuf_ref, vbuf_ref, sem_ref, *, tq, topk):
    H, dh = DSA_HEADS, DSA_DH
    hd = H * dh
    b_id, qi = pl.program_id(0), pl.program_id(1)
    n_chunks = qi + 2
    krow = lax.broadcasted_iota(I32, (tq, tq), 0)
    qcol = lax.broadcasted_iota(I32, (tq, tq), 1)

    def k_copy_meta():
        return pltpu.make_async_copy(km_hbm, kbuf_ref.at[0], sem_ref.at[0, 0])

    def chunk_span(c):
        start = c * tq
        return pl.ds(start if isinstance(start, int) else pl.multiple_of(start, tq), tq)

    def k_copy(c, par):
        return pltpu.make_async_copy(z_hbm.at[b_id, chunk_span(c - 1), pl.ds(ZG_W + hd, hd)], kbuf_ref.at[par],
                                     sem_ref.at[0, par])

    def v_copy(c, par):
        return pltpu.make_async_copy(vt_hbm.at[b_id, :, chunk_span(c)], vbuf_ref.at[par], sem_ref.at[1, par])

    k_copy_meta().start()

    @pl.when((b_id == 0) & (qi == 0))
    def _():
        for t, dist in enumerate((N_META + qcol - krow, qcol - krow, tq + qcol - krow)):
            for h in range(H):
                far = relb_ref[REL_BUCKETS - 1, h]
                tile = jnp.full((tq, tq), (relb_ref[0, h] - far) * LOG2E, F32)
                for bkt in range(1, REL_BUCKETS):
                    tile = jnp.where(dist >= BUCKET_START[bkt], (relb_ref[bkt, h] - far) * LOG2E, tile)
                bias_ref[t, h] = tile

    def score_and_select():
        zi = zi_ref[0]
        m_ref[...] = jnp.full(m_ref.shape, -1e30, F32)
        acc_ref[...] = jnp.zeros(acc_ref.shape, F32)

        iq = [zi[:, h * IDX_DH:(h + 1) * IDX_DH].astype(BF16) for h in range(IDX_HEADS)]
        small_t = zi[:, IQ_W:].T
        iw = [small_t[IW_OFF + h:IW_OFF + h + 1, :] for h in range(IDX_HEADS)]

        def score_keys(ci):
            ikc = ik_ref[0, ci]
            sc = jnp.zeros((tq, tq), F32)
            for h in range(IDX_HEADS):
                sc += iw[h] * jnp.maximum(_dot_nt(ikc, iq[h]), 0.0)
            return _sort_key(sc)

        def store_keys(ci, key):
            skey_ref[ci] = key
            hi_ref[ci] = lax.shift_right_arithmetic(key, 16).astype(I16)

        store_keys(0, jnp.where(krow < N_META, score_keys(0), INT_MIN))

        def full_chunk(ci, carry):
            store_keys(ci, score_keys(ci))
            return carry

        lax.fori_loop(1, qi + 1, full_chunk, 0)
        store_keys(qi + 1, jnp.where(krow <= qcol, score_keys(qi + 1), INT_MIN))

        n_acc = 4

        def count16(ref, cand, strict=False):
            cand16 = jnp.broadcast_to(cand, (BF16_SUBLANES, tq)).astype(I16)

            def body(ci, accs):
                accs = list(accs)
                for j in range(tq // BF16_SUBLANES):
                    blk = ref[ci, j * BF16_SUBLANES:(j + 1) * BF16_SUBLANES, :]
                    hit = blk > cand16 if strict else blk >= cand16
                    accs[j % n_acc] = accs[j % n_acc] + jnp.where(hit, ONE16, ZERO16)
                return tuple(accs)

            accs = lax.fori_loop(0, n_chunks, body,
                                 tuple(jnp.zeros((BF16_SUBLANES, tq), I16) for _ in range(n_acc)))
            tot = sum((a.astype(I32) for a in accs[1:]), accs[0].astype(I32))
            return jnp.sum(tot, axis=0, keepdims=True)

        def search16(ref):
            def step(p, u):
                cand = u | lax.shift_left(jnp.int32(1), 15 - p)
                return jnp.where(count16(ref, cand - HALF16) >= topk, cand, u)
            return lax.fori_loop(0, 16, step, jnp.zeros((1, tq), I32))

        hi_thr = search16(hi_ref) - HALF16

        def low_halves(ci, carry):
            key = skey_ref[ci]
            hi = lax.shift_right_arithmetic(key, 16)
            lo = (key & 0xFFFF) - HALF16
            lo = jnp.where(hi == hi_thr, lo, jnp.where(hi > hi_thr, HALF16 - 1, -HALF16))
            lo_ref[ci] = lo.astype(I16)
            return carry

        lax.fori_loop(0, n_chunks, low_halves, 0)
        lo_thr = search16(lo_ref) - HALF16
        prefix = hi_thr * (2 * HALF16) + (lo_thr + HALF16)
        thr = jnp.maximum(prefix, INT_MIN + 1)
        thr_ref[...] = thr

        n_ge = jnp.where(prefix > INT_MIN, count16(lo_ref, lo_thr), 0)
        n_gt = count16(lo_ref, lo_thr, strict=True)

        @pl.when(jnp.max(n_ge) > topk)
        def _():
            need = (topk - n_gt).astype(F32)
            incl = (qcol <= krow).astype(BF16)

            def fix(ci, seen):
                sk = skey_ref[ci]
                tie = sk == thr
                tie_f = jnp.where(tie, 1.0, 0.0)
                run = jnp.dot(incl, tie_f.astype(BF16), preferred_element_type=F32)
                drop = tie & (seen + run > need)
                skey_ref[ci] = jnp.where(drop, INT_MIN, sk)
                return seen + jnp.sum(tie_f, axis=0, keepdims=True)

            lax.fori_loop(0, n_chunks, fix, jnp.zeros((1, tq), F32))

    def logits_stage(par, h, bias_idx):
        hs = slice(h * dh, (h + 1) * dh)
        s = _dot_nt(kbuf_ref[par, :, hs], q_ref[0, :, hs]) + neg_ref[...]
        if bias_idx is not None:
            s += bias_ref[bias_idx, h]
        s_ref[par, h] = s
        cm_ref[par, h] = jnp.max(s, axis=0, keepdims=True)

    def value_stage(par, h):
        m_old = m_ref[h]
        m_new = jnp.maximum(m_old, cm_ref[par, h])
        p = jnp.exp2(s_ref[par, h] - m_new).astype(BF16)
        pv = jnp.dot(vbuf_ref[par, h * VROWS:(h + 1) * VROWS, :], p, preferred_element_type=F32)
        acc_ref[h] = jnp.exp2(m_old - m_new) * acc_ref[h] + pv
        m_ref[h] = m_new

    def pipeline_step(c, par, bias_idx, first=False, last=False):
        (k_copy_meta() if first else k_copy(c, par)).wait()
        if not first:
            v_copy(c - 1, 1 - par).wait()
        if not last:
            k_copy(c + 1, 1 - par).start()
        v_copy(c, par).start()
        neg_ref[...] = jnp.where(skey_ref[c] >= thr_ref[...], 0.0, -jnp.inf)
        for h in range(H):
            logits_stage(par, h, bias_idx)
            if not first:
                value_stage(1 - par, h)

    def drain(c, par):
        v_copy(c, par).wait()
        for h in range(H):
            value_stage(par, h)

    score_and_select()

    @pl.when(qi == 0)
    def _():
        pipeline_step(0, 0, 0, first=True)

    @pl.when(qi > 0)
    def _():
        pipeline_step(0, 0, None, first=True)

    def far_pair(i, carry):
        pipeline_step(2 * i + 1, 1, None)
        pipeline_step(2 * i + 2, 0, None)
        return carry

    lax.fori_loop(0, (qi - 1) // 2, far_pair, 0)
    qi_odd = qi % 2 == 1

    @pl.when((qi >= 2) & ~qi_odd)
    def _():
        pipeline_step(qi - 1, 1, None)

    @pl.when(qi_odd)
    def _():
        pipeline_step(qi, 1, 2)
        pipeline_step(qi + 1, 0, 1, last=True)
        drain(qi + 1, 0)

    @pl.when((qi >= 2) & ~qi_odd)
    def _():
        pipeline_step(qi, 0, 2)

    @pl.when(~qi_odd)
    def _():
        pipeline_step(qi + 1, 1, 1, last=True)
        drain(qi + 1, 1)

    for h in range(H):
        out_t = acc_ref[h, :dh, :] / acc_ref[h, dh:dh + 1, :]
        yb_ref[0, :, h * dh:(h + 1) * dh] = out_t.T.astype(yb_ref.dtype)


def _dsa(z3, vt, zi3, k_meta, ik, rel_bias, *, tq, topk):
    nb, seq, _ = z3.shape
    H, dh = DSA_HEADS, DSA_DH
    nq = seq // tq
    n_chunks = nq + 1
    hd = H * dh
    q_blk = ZG_W // hd
    hbm = pl.BlockSpec(memory_space=pl.ANY)
    return pl.pallas_call(
        functools.partial(_dsa_kernel, tq=tq, topk=topk),
        grid=(nb, nq),
        in_specs=[pl.BlockSpec((1, tq, hd), lambda b, i: (b, i, q_blk)),
                  pl.BlockSpec((1, tq, IQ_W + SMALL_W), lambda b, i: (b, i, 0)),
                  pl.BlockSpec((1, n_chunks, tq, IDX_DH), lambda b, i: (b, 0, 0, 0)),
                  pl.BlockSpec(memory_space=pltpu.SMEM),
                  hbm, hbm, hbm],
        out_specs=pl.BlockSpec((1, tq, hd), lambda b, i: (b, i, 0)),
        out_shape=jax.ShapeDtypeStruct((nb, seq, hd), BF16),
        scratch_shapes=[pltpu.VMEM((n_chunks, tq, tq), I32),
                        pltpu.VMEM((n_chunks, tq, tq), I16),
                        pltpu.VMEM((n_chunks, tq, tq), I16),
                        pltpu.VMEM((1, tq), I32),
                        pltpu.VMEM((tq, tq), F32),
                        pltpu.VMEM((2, H, tq, tq), F32),
                        pltpu.VMEM((2, H, 1, tq), F32),
                        pltpu.VMEM((H, 1, tq), F32),
                        pltpu.VMEM((H, VROWS, tq), F32),
                        pltpu.VMEM((3, H, tq, tq), F32),
                        pltpu.VMEM((2, tq, hd), BF16),
                        pltpu.VMEM((2, H * VROWS, tq), BF16),
                        pltpu.SemaphoreType.DMA((2, 2))],
        compiler_params=_params("arbitrary", "arbitrary"),
        name="dsa",
    )(z3, zi3, ik, rel_bias, k_meta, z3, vt)


def _merge_kernel(h_ref, ya_ref, yb_ref, g_ref, wa_ref, wb_ref, wo_ref, o_ref):
    d = h_ref.shape[1]
    g = g_ref[...].astype(F32)
    ua = jnp.dot(ya_ref[...], wa_ref[...], preferred_element_type=F32)
    ub = jnp.dot(yb_ref[...], wb_ref[...], preferred_element_type=F32)
    merged = _sigmoid(g[:, :d]) * ua + _sigmoid(g[:, d:]) * ub
    o_ref[...] = h_ref[...] + jnp.dot(merged.astype(BF16), wo_ref[...], preferred_element_type=F32)


def _merge(h, ya, yb, z, wa, wb, wo, *, tm):
    n, d = h.shape
    gate_blk = (ZG_W + ZD_W) // (2 * d)
    row = pl.BlockSpec((tm, d), lambda i: (i, 0))
    wspec = pl.BlockSpec((d, d), lambda i: (0, 0))
    return pl.pallas_call(
        _merge_kernel,
        grid=(n // tm,),
        in_specs=[row, row, row, pl.BlockSpec((tm, 2 * d), lambda i: (i, gate_blk)), wspec, wspec, wspec],
        out_specs=row,
        out_shape=jax.ShapeDtypeStruct((n, d), F32),
        compiler_params=_params("parallel"),
        name="merge",
    )(h, ya, yb, z, wa, wb, wo)


def kernel(x, meta_tokens, norm_ffn1, ffn1_w_gate, ffn1_w_up, ffn1_w_down, norm_mix, w_in, gla_w_lr, gla_b_lr,
           gla_norm, w_up_a, w_up_b, w_out, rel_bias, norm_ffn2, ffn2_w_gate, ffn2_w_up, ffn2_w_down, norm_final):
    nb, seq, d = x.shape
    assert norm_ffn1.shape[0] == 1, "single-layer block only"
    assert meta_tokens.shape[0] == N_META
    tq = 256
    assert seq % tq == 0 and (ZG_W + ZD_W) % (2 * d) == 0 and ZG_W % (DSA_HEADS * DSA_DH) == 0
    topk = min(TOPK_MAX, seq // 4)
    dff = ffn1_w_gate.shape[2]
    tf = dff // 2
    bf = lambda a: a.astype(BF16)

    sizes = (GLA_HEADS * GLA_DK, GLA_HEADS * GLA_DK, GLA_HEADS * GLA_DV, GLA_HEADS * GLA_DV, GLA_LOWRANK,
             DSA_HEADS * DSA_DH, DSA_HEADS * DSA_DH, DSA_HEADS * DSA_DH, IQ_W, IDX_DH, IDX_HEADS, d, d)
    offs = np.concatenate([[0], np.cumsum(sizes)])
    (c_gq, c_gk, c_gv, c_gr, c_lr, c_dq, c_dk, c_dv, c_iq, c_ik, c_iw, c_ga, c_gb) = [
        w_in[0][:, offs[i]:offs[i + 1]] for i in range(len(sizes))]
    c_gq = c_gq * (GLA_DK ** -0.5)
    c_dq = c_dq * (DSA_DH ** -0.5 * LOG2E)
    c_iw = c_iw * ((IDX_HEADS ** -0.5) * (IDX_DH ** -0.5))
    w_big = bf(jnp.concatenate([c_gq, c_gk, c_gv, c_gr, c_dq, c_dk, c_dv, c_ga, c_gb], axis=1))
    pad_w = SMALL_W - (IDX_DH + IDX_HEADS + GLA_LOWRANK)
    w_small = bf(jnp.concatenate([c_iq, c_ik, c_iw, c_lr, jnp.zeros((d, pad_w), F32)], axis=1))
    wlr_pad = jnp.zeros((SMALL_W, GLA_HEADS * GLA_DK), F32).at[LR_OFF:LR_OFF + GLA_LOWRANK].set(gla_w_lr[0])

    xr = x.reshape(nb * seq, d)
    g1, gm = norm_ffn1, norm_mix
    wg1, wu1, wd1 = bf(ffn1_w_gate[0]), bf(ffn1_w_up[0]), bf(ffn1_w_down[0])

    h1 = _ffn(xr, g1, wg1, wu1, wd1, tm=512, tf=tf)
    h1m = _ffn(meta_tokens, g1, wg1, wu1, wd1, tm=512, tf=tf)
    z = _proj(h1, gm, w_big, BF16, tm=512, tn=2048, name="proj_big")
    zi = _proj(h1, gm, w_small, F32, tm=512, tn=IQ_W + SMALL_W, name="proj_small")
    zm = _proj(h1m, gm, w_big, BF16, tm=512, tn=2048, name="proj_big_meta")
    zim = _proj(h1m, gm, w_small, F32, tm=512, tn=IQ_W + SMALL_W, name="proj_small_meta")

    z3 = z.reshape(nb, seq, z.shape[1])
    zi3 = zi.reshape(nb, seq, zi.shape[1])

    ya = _gla(z3, zi3, zm, zim, wlr_pad, gla_b_lr, gla_norm, chunk=128)

    hd = DSA_HEADS * DSA_DH
    n_chunks = seq // tq + 1
    meta_chunk = lambda a: jnp.pad(a, ((0, tq - N_META), (0, 0)))
    with_meta = lambda m, xs: jnp.concatenate([jnp.broadcast_to(meta_chunk(m)[None], (nb, tq, m.shape[1])), xs], 1)
    k_meta = meta_chunk(zm[:, ZG_W + hd:ZG_W + 2 * hd])
    v_all = with_meta(zm[:, ZG_W + 2 * hd:ZG_W + 3 * hd], z3[:, :, ZG_W + 2 * hd:ZG_W + 3 * hd])
    vt4 = jnp.swapaxes(v_all, 1, 2).reshape(nb, DSA_HEADS, DSA_DH, n_chunks * tq)
    vt = jnp.concatenate([vt4, jnp.ones((nb, DSA_HEADS, BF16_SUBLANES, n_chunks * tq), BF16)], axis=2)
    vt = vt.reshape(nb, DSA_HEADS * VROWS, n_chunks * tq)
    ik_all = with_meta(zim[:, IQ_W + IK_OFF:IQ_W + IK_OFF + IDX_DH], zi3[:, :, IQ_W + IK_OFF:IQ_W + IK_OFF + IDX_DH])
    ik = bf(ik_all.reshape(nb, n_chunks, tq, IDX_DH))
    yb = _dsa(z3, vt, zi3, k_meta, ik, rel_bias, tq=tq, topk=topk)

    h2 = _merge(h1, ya.reshape(nb * seq, -1), yb.reshape(nb * seq, -1), z,
                bf(w_up_a[0]), bf(w_up_b[0]), bf(w_out[0]), tm=512)
    out = _ffn(h2, norm_ffn2, bf(ffn2_w_gate[0]), bf(ffn2_w_up[0]), bf(ffn2_w_down[0]), norm_final[None, :],
               tm=512, tf=tf)
    return out.reshape(nb, seq, d)
```

```python
import functools

import numpy as np
import jax
import jax.numpy as jnp
from jax import lax
from jax.experimental import pallas as pl
from jax.experimental.pallas import tpu as pltpu

F32 = jnp.float32
BF16 = jnp.bfloat16
I32 = jnp.int32
HIGHEST = lax.Precision.HIGHEST

N_META = 16
EPS = 1e-6
GLA_HEADS, GLA_DK, GLA_DV, GLA_LOWRANK, GLA_TAU = 4, 128, 256, 16, 16.0
DSA_HEADS, DSA_DH, IDX_HEADS, IDX_DH, TOPK_MAX = 8, 128, 8, 64, 256
REL_BUCKETS, REL_MAX_DIST = 32, 128

LANES = 128
SUBLANES = 8
VMEM_LIMIT = 56 * 1024 * 1024
INT_MIN = -(2 ** 31)
LOG2E = float(np.log2(np.e))
BF16_SUBLANES = 16
VROWS = DSA_DH + BF16_SUBLANES
KV_SLOTS = 8
KV_AHEAD = 5
assert KV_AHEAD + 2 <= KV_SLOTS
DSA_LOGIT_BUFFERS = 8

ZG_W = 2 * GLA_HEADS * GLA_DK + 2 * GLA_HEADS * GLA_DV
ZD_W = 3 * DSA_HEADS * DSA_DH
SMALL_W = LANES
IQ_W = IDX_HEADS * IDX_DH
IK_OFF, IW_OFF, LR_OFF = 0, IDX_DH, IDX_DH + IDX_HEADS


def _rel_bucket_thresholds():
    n = np.arange(0, 4 * REL_MAX_DIST)
    max_exact = REL_BUCKETS // 2
    nf = np.maximum(n, max_exact).astype(np.float32)
    large = max_exact + (np.log(nf / np.float32(max_exact)) / np.float32(np.log(REL_MAX_DIST / max_exact))
                         * np.float32(REL_BUCKETS - max_exact)).astype(np.int32)
    large64 = max_exact + (np.log(np.maximum(n, max_exact) / max_exact) / np.log(REL_MAX_DIST / max_exact)
                           * (REL_BUCKETS - max_exact)).astype(np.int64)
    assert (large == large64).all()
    bucket = np.where(n < max_exact, n, np.minimum(large, REL_BUCKETS - 1))
    assert (np.diff(bucket) >= 0).all()
    return [int(np.argmax(bucket >= b)) for b in range(REL_BUCKETS)]


BUCKET_START = _rel_bucket_thresholds()


def _rms(x, g):
    return x * lax.rsqrt(jnp.mean(x * x, axis=-1, keepdims=True) + EPS) * g


def _sigmoid(x):
    return 1.0 / (1.0 + jnp.exp(-x))


def _dot_nt(a, b):
    return lax.dot_general(a, b, (((1,), (1,)), ((), ())), preferred_element_type=F32)


def _params(*sem):
    return pltpu.CompilerParams(dimension_semantics=sem, vmem_limit_bytes=VMEM_LIMIT)


def _ffn_kernel(*refs, final_norm):
    if final_norm:
        x_ref, g_ref, wg_ref, wu_ref, wd_ref, gf_ref, o_ref, xn_ref = refs
    else:
        x_ref, g_ref, wg_ref, wu_ref, wd_ref, o_ref, xn_ref = refs
    j = pl.program_id(1)

    @pl.when(j == 0)
    def _():
        xn_ref[...] = _rms(x_ref[...], g_ref[...]).astype(BF16)
        o_ref[...] = jnp.zeros_like(o_ref)

    xn = xn_ref[...]
    a = jnp.dot(xn, wg_ref[...], preferred_element_type=F32)
    b = jnp.dot(xn, wu_ref[...], preferred_element_type=F32)
    hmid = (a * _sigmoid(a) * b).astype(BF16)
    o_ref[...] += jnp.dot(hmid, wd_ref[...], preferred_element_type=F32)

    @pl.when(j == pl.num_programs(1) - 1)
    def _():
        h = x_ref[...] + 0.5 * o_ref[...]
        if final_norm:
            h = _rms(h, gf_ref[...])
        o_ref[...] = h


def _ffn(x, g, wg, wu, wd, gf=None, *, tm, tf):
    n, d = x.shape
    dff = wg.shape[1]
    tm = min(tm, n)
    row = pl.BlockSpec((tm, d), lambda i, j: (i, 0))
    vec = pl.BlockSpec((1, d), lambda i, j: (0, 0))
    in_specs = [row, vec,
                pl.BlockSpec((d, tf), lambda i, j: (0, j)),
                pl.BlockSpec((d, tf), lambda i, j: (0, j)),
                pl.BlockSpec((tf, d), lambda i, j: (j, 0))]
    args = [x, g, wg, wu, wd]
    if gf is not None:
        in_specs.append(vec)
        args.append(gf)
    return pl.pallas_call(
        functools.partial(_ffn_kernel, final_norm=gf is not None),
        grid=(n // tm, dff // tf),
        in_specs=in_specs,
        out_specs=row,
        out_shape=jax.ShapeDtypeStruct((n, d), F32),
        scratch_shapes=[pltpu.VMEM((tm, d), BF16)],
        compiler_params=_params("parallel", "arbitrary"),
        name="ffn_final" if gf is not None else "ffn",
    )(*args)


def _proj_kernel(x_ref, g_ref, w_ref, o_ref, xn_ref):
    @pl.when(pl.program_id(1) == 0)
    def _():
        xn_ref[...] = _rms(x_ref[...], g_ref[...]).astype(BF16)

    o_ref[...] = jnp.dot(xn_ref[...], w_ref[...], preferred_element_type=F32).astype(o_ref.dtype)


def _proj(x, g, w, out_dtype, *, tm, tn, name):
    n, d = x.shape
    width = w.shape[1]
    tm = min(tm, n)
    tn = min(tn, width)
    return pl.pallas_call(
        _proj_kernel,
        grid=(n // tm, width // tn),
        in_specs=[pl.BlockSpec((tm, d), lambda i, j: (i, 0)),
                  pl.BlockSpec((1, d), lambda i, j: (0, 0)),
                  pl.BlockSpec((d, tn), lambda i, j: (0, j))],
        out_specs=pl.BlockSpec((tm, tn), lambda i, j: (i, j)),
        out_shape=jax.ShapeDtypeStruct((n, width), out_dtype),
        scratch_shapes=[pltpu.VMEM((tm, d), BF16)],
        compiler_params=_params("parallel", "arbitrary"),
        name=name,
    )(x, g, w)


def _lower_tri(n):
    r = lax.broadcasted_iota(I32, (n, n), 0)
    c = lax.broadcasted_iota(I32, (n, n), 1)
    return c <= r


def _gla_kernel(zg_ref, zs_ref, zgm_ref, zsm_ref, wlr_ref, blr_ref, gn_ref, ya_ref, st_ref, *, nb, chunk):
    H, dk, dv = GLA_HEADS, GLA_DK, GLA_DV
    kw = H * dk
    wlr = wlr_ref[...]
    blr = blr_ref[...]
    gn = gn_ref[...]

    def log_decay(zs):
        x = jnp.dot(zs, wlr, preferred_element_type=F32, precision=HIGHEST) + blr
        return (jnp.minimum(x, 0.0) - jnp.log(1.0 + jnp.exp(-jnp.abs(x)))) * (1.0 / GLA_TAU)

    def cumsum_rows(g):
        tri = _lower_tri(g.shape[0]).astype(F32)
        return jnp.dot(tri, g, preferred_element_type=F32, precision=HIGHEST)

    @pl.when(pl.program_id(0) == 0)
    def _():
        bc = cumsum_rows(log_decay(zsm_ref[...]))
        bl = bc[N_META - 1:N_META, :]
        zgm = zgm_ref[...]
        kl = (zgm[:, kw:2 * kw].astype(F32) * jnp.exp(bl - bc)).astype(BF16)
        v = zgm[:, 2 * kw:2 * kw + H * dv]
        for h in range(H):
            st = lax.dot_general(v[:, h * dv:(h + 1) * dv], kl[:, h * dk:(h + 1) * dk],
                                 (((0,), (0,)), ((), ())), preferred_element_type=F32)
            for b in range(nb):
                st_ref[b * H + h] = st

    causal = _lower_tri(chunk)
    mid = chunk // 2
    for b in range(nb):
        zg = zg_ref[b]
        bc = cumsum_rows(log_decay(zs_ref[b]))
        bl = bc[chunk - 1:chunk, :]
        bm = bc[mid - 1:mid, :]
        qe = zg[:, :kw].astype(F32) * jnp.exp(bc - bm)
        ke = zg[:, kw:2 * kw].astype(F32) * jnp.exp(bm - bc)
        qs = (qe * jnp.exp(bm)).astype(BF16)
        kl = (ke * jnp.exp(bl - bm)).astype(BF16)
        qe = qe.astype(BF16)
        ke = ke.astype(BF16)
        dec = jnp.exp(bl)
        for h in range(H):
            ks = slice(h * dk, (h + 1) * dk)
            v = zg[:, 2 * kw + h * dv:2 * kw + (h + 1) * dv]
            r = zg[:, 2 * kw + H * dv + h * dv:2 * kw + H * dv + (h + 1) * dv].astype(F32)
            a = jnp.where(causal, _dot_nt(qe[:, ks], ke[:, ks]), 0.0).astype(BF16)
            st = st_ref[b * H + h]
            o = jnp.dot(a, v, preferred_element_type=F32) + _dot_nt(qs[:, ks], st.astype(BF16))
            st_ref[b * H + h] = st * dec[:, ks] + lax.dot_general(
                v, kl[:, ks], (((0,), (0,)), ((), ())), preferred_element_type=F32)
            y = _rms(o, gn) * (r * _sigmoid(r))
            ya_ref[b, :, h * dv:(h + 1) * dv] = y.astype(ya_ref.dtype)


def _gla(z3, zi3, z_meta, zi_meta, wlr_pad, blr, gn, *, chunk):
    nb, seq, _ = z3.shape
    H, dk, dv = GLA_HEADS, GLA_DK, GLA_DV
    small_blk = IQ_W // SMALL_W
    return pl.pallas_call(
        functools.partial(_gla_kernel, nb=nb, chunk=chunk),
        grid=(seq // chunk,),
        in_specs=[pl.BlockSpec((nb, chunk, ZG_W), lambda c: (0, c, 0)),
                  pl.BlockSpec((nb, chunk, SMALL_W), lambda c: (0, c, small_blk)),
                  pl.BlockSpec((N_META, ZG_W), lambda c: (0, 0)),
                  pl.BlockSpec((N_META, SMALL_W), lambda c: (0, small_blk)),
                  pl.BlockSpec((SMALL_W, H * dk), lambda c: (0, 0)),
                  pl.BlockSpec((1, H * dk), lambda c: (0, 0)),
                  pl.BlockSpec((1, dv), lambda c: (0, 0))],
        out_specs=pl.BlockSpec((nb, chunk, H * dv), lambda c: (0, c, 0)),
        out_shape=jax.ShapeDtypeStruct((nb, seq, H * dv), BF16),
        scratch_shapes=[pltpu.VMEM((nb * H, dv, dk), F32)],
        compiler_params=_params("arbitrary"),
        name="gla",
    )(z3, zi3, z_meta, zi_meta, wlr_pad, blr, gn)


def _dsa_kernel(q_ref, zi_ref, ik_ref, relb_ref, km_hbm, z_hbm, vt_hbm, yb_ref,
                sc_ref, thr_ref, neg_ref, s_ref, cm_ref, m_ref, acc_ref, bias_ref,
                kbuf_ref, vbuf_ref, sem_ref, *, tq, topk):
    H, dh = DSA_HEADS, DSA_DH
    hd = H * dh
    b_id, qi = pl.program_id(0), pl.program_id(1)
    n_chunks = qi + 2
    krow = lax.broadcasted_iota(I32, (tq, tq), 0)
    qcol = lax.broadcasted_iota(I32, (tq, tq), 1)

    last_chunk = qi + 1

    def k_copy(c):
        slot = c % KV_SLOTS
        if isinstance(c, int) and c == 0:
            src = km_hbm
        else:
            start = (c - 1) * tq
            rows = pl.ds(start if isinstance(start, int) else pl.multiple_of(start, tq), tq)
            src = z_hbm.at[b_id, rows, pl.ds(ZG_W + hd, hd)]
        return pltpu.make_async_copy(src, kbuf_ref.at[slot], sem_ref.at[0, slot])

    def v_copy(c):
        slot = c % KV_SLOTS
        return pltpu.make_async_copy(vt_hbm.at[b_id, c], vbuf_ref.at[slot], sem_ref.at[1, slot])

    def kv_fetch(c):
        k_copy(c).start()
        v_copy(c).start()

    kv_fetch(0)
    kv_fetch(1)
    for c in range(2, min(KV_AHEAD, ik_ref.shape[1] - 1) + 1):
        pl.when(c <= last_chunk)(functools.partial(kv_fetch, c))

    @pl.when((b_id == 0) & (qi == 0))
    def _():
        for t, dist in enumerate((N_META + qcol - krow, qcol - krow, tq + qcol - krow)):
            for h in range(H):
                far = relb_ref[REL_BUCKETS - 1, h]
                tile = jnp.full((tq, tq), (relb_ref[0, h] - far) * LOG2E, F32)
                for bkt in range(1, REL_BUCKETS):
                    tile = jnp.where(dist >= BUCKET_START[bkt], (relb_ref[bkt, h] - far) * LOG2E, tile)
                bias_ref[t, h] = tile

    def score_and_select():
        zi = zi_ref[0]
        m_ref[...] = jnp.full(m_ref.shape, -1e30, F32)
        acc_ref[...] = jnp.zeros(acc_ref.shape, F32)

        iq = [zi[:, h * IDX_DH:(h + 1) * IDX_DH].astype(BF16) for h in range(IDX_HEADS)]
        small_t = zi[:, IQ_W:].T
        iw = [small_t[IW_OFF + h:IW_OFF + h + 1, :] for h in range(IDX_HEADS)]

        def scores(ci):
            ikc = ik_ref[0, ci]
            sc = jnp.zeros((tq, tq), F32)
            for h in range(IDX_HEADS):
                sc += iw[h] * jnp.maximum(_dot_nt(ikc, iq[h]), 0.0)
            return sc

        sc_ref[0] = jnp.where(krow < N_META, scores(0), -jnp.inf)

        def full_chunk(ci, carry):
            sc_ref[ci] = scores(ci)
            return carry

        lax.fori_loop(1, qi + 1, full_chunk, 0)
        sc_ref[qi + 1] = jnp.where(krow <= qcol, scores(qi + 1), -jnp.inf)

        def as_float(key):
            return pltpu.bitcast(key ^ (lax.shift_right_arithmetic(key, 31) & 0x7FFFFFFF), F32)

        def count(cand, strict=False):
            cand8 = jnp.broadcast_to(cand, (SUBLANES, tq))

            def body(ci, accs):
                accs = list(accs)
                for j in range(tq // SUBLANES):
                    blk = sc_ref[ci, j * SUBLANES:(j + 1) * SUBLANES, :]
                    hit = blk > cand8 if strict else blk >= cand8
                    accs[j % len(accs)] = accs[j % len(accs)] + jnp.where(hit, 1, 0)
                return tuple(accs)

            accs = lax.fori_loop(0, n_chunks, body, tuple(jnp.zeros((SUBLANES, tq), I32) for _ in range(4)))
            return jnp.sum(sum(accs[1:], accs[0]), axis=0, keepdims=True)

        def search(p, prefix):
            cand = prefix ^ lax.shift_left(jnp.int32(1), 31 - p)
            return jnp.where(count(as_float(cand)) >= topk, cand, prefix)

        prefix = lax.fori_loop(0, 32, search, jnp.full((1, tq), INT_MIN, I32))
        thr = jnp.where(prefix == INT_MIN, jnp.finfo(F32).min, as_float(prefix))
        thr_ref[...] = thr

        n_ge = jnp.where(prefix == INT_MIN, 0, count(thr))

        @pl.when(jnp.max(n_ge) > topk)
        def _():
            need = (topk - count(thr, strict=True)).astype(F32)
            incl = (qcol <= krow).astype(BF16)

            def fix(ci, seen):
                sc = sc_ref[ci]
                tie = sc == thr
                tie_f = jnp.where(tie, 1.0, 0.0)
                run = jnp.dot(incl, tie_f.astype(BF16), preferred_element_type=F32)
                drop = tie & (seen + run > need)
                sc_ref[ci] = jnp.where(drop, -jnp.inf, sc)
                return seen + jnp.sum(tie_f, axis=0, keepdims=True)

            lax.fori_loop(0, n_chunks, fix, jnp.zeros((1, tq), F32))


    def logits_stage(c, par, h, bias_idx):
        hs = slice(h * dh, (h + 1) * dh)
        s = _dot_nt(kbuf_ref[c % KV_SLOTS, :, hs], q_ref[0, :, hs]) + neg_ref[...]
        if bias_idx is not None:
            s += bias_ref[bias_idx, h]
        s_ref[par, h] = s
        cm_ref[par, h] = jnp.max(s, axis=0, keepdims=True)

    def value_stage(c, par, h):
        m_old = m_ref[h]
        m_new = jnp.maximum(m_old, cm_ref[par, h])
        p = jnp.exp2(s_ref[par, h] - m_new).astype(BF16)
        pv = jnp.dot(vbuf_ref[c % KV_SLOTS, h * VROWS:(h + 1) * VROWS, :], p, preferred_element_type=F32)
        acc_ref[h] = jnp.exp2(m_old - m_new) * acc_ref[h] + pv
        m_ref[h] = m_new

    def pipeline_step(c, par, bias_idx, first=False):
        k_copy(c).wait()
        if not first:
            v_copy(c - 1).wait()
            pl.when(c + KV_AHEAD <= last_chunk)(functools.partial(kv_fetch, c + KV_AHEAD))
        neg_ref[...] = jnp.where(sc_ref[c] >= thr_ref[...], 0.0, -jnp.inf)
        for h in range(H):
            logits_stage(c, par, h, bias_idx)
            if not first:
                value_stage(c - 1, 1 - par, h)

    def drain(c, par):
        v_copy(c).wait()
        for h in range(H):
            value_stage(c, par, h)

    score_and_select()

    @pl.when(qi == 0)
    def _():
        pipeline_step(0, 0, 0, first=True)

    @pl.when(qi > 0)
    def _():
        pipeline_step(0, 0, None, first=True)

    def far_pair(i, carry):
        pipeline_step(2 * i + 1, 1, None)
        pipeline_step(2 * i + 2, 0, None)
        return carry

    lax.fori_loop(0, (qi - 1) // 2, far_pair, 0)
    qi_odd = qi % 2 == 1

    @pl.when((qi >= 2) & ~qi_odd)
    def _():
        pipeline_step(qi - 1, 1, None)

    @pl.when(qi_odd)
    def _():
        pipeline_step(qi, 1, 2)
        pipeline_step(qi + 1, 0, 1)
        drain(qi + 1, 0)

    @pl.when((qi >= 2) & ~qi_odd)
    def _():
        pipeline_step(qi, 0, 2)

    @pl.when(~qi_odd)
    def _():
        pipeline_step(qi + 1, 1, 1)
        drain(qi + 1, 1)

    for h in range(H):
        out_t = acc_ref[h, :dh, :] / acc_ref[h, dh:dh + 1, :]
        yb_ref[0, :, h * dh:(h + 1) * dh] = out_t.T.astype(yb_ref.dtype)


def _dsa(z3, vt, zi3, k_meta, ik, rel_bias, *, tq, topk):
    nb, seq, _ = z3.shape
    H, dh = DSA_HEADS, DSA_DH
    nq = seq // tq
    n_chunks = nq + 1
    hd = H * dh
    q_blk = ZG_W // hd
    hbm = pl.BlockSpec(memory_space=pl.ANY)
    return pl.pallas_call(
        functools.partial(_dsa_kernel, tq=tq, topk=topk),
        grid=(nb, nq),
        in_specs=[pl.BlockSpec((1, tq, hd), lambda b, i: (b, i, q_blk)),
                  pl.BlockSpec((1, tq, IQ_W + SMALL_W), lambda b, i: (b, i, 0)),
                  pl.BlockSpec((1, n_chunks, tq, IDX_DH), lambda b, i: (b, 0, 0, 0)),
                  pl.BlockSpec(memory_space=pltpu.SMEM),
                  hbm, hbm, hbm],
        out_specs=pl.BlockSpec((1, tq, hd), lambda b, i: (b, i, 0)),
        out_shape=jax.ShapeDtypeStruct((nb, seq, hd), BF16),
        scratch_shapes=[pltpu.VMEM((n_chunks, tq, tq), F32),
                        pltpu.VMEM((1, tq), F32),
                        pltpu.VMEM((tq, tq), F32),
                        pltpu.VMEM((2, H, tq, tq), F32),
                        pltpu.VMEM((2, H, 1, tq), F32),
                        pltpu.VMEM((H, 1, tq), F32),
                        pltpu.VMEM((H, VROWS, tq), F32),
                        pltpu.VMEM((3, H, tq, tq), F32),
                        pltpu.VMEM((KV_SLOTS, tq, hd), BF16),
                        pltpu.VMEM((KV_SLOTS, H * VROWS, tq), BF16),
                        pltpu.SemaphoreType.DMA((2, KV_SLOTS))],
        compiler_params=_params("arbitrary", "arbitrary"),
        name="dsa",
    )(z3, zi3, ik, rel_bias, k_meta, z3, vt)


def _merge_kernel(h_ref, ya_ref, yb_ref, g_ref, wa_ref, wb_ref, wo_ref, o_ref):
    d = h_ref.shape[1]
    g = g_ref[...].astype(F32)
    ua = jnp.dot(ya_ref[...], wa_ref[...], preferred_element_type=F32)
    ub = jnp.dot(yb_ref[...], wb_ref[...], preferred_element_type=F32)
    merged = _sigmoid(g[:, :d]) * ua + _sigmoid(g[:, d:]) * ub
    o_ref[...] = h_ref[...] + jnp.dot(merged.astype(BF16), wo_ref[...], preferred_element_type=F32)


def _merge(h, ya, yb, z, wa, wb, wo, *, tm):
    n, d = h.shape
    gate_blk = (ZG_W + ZD_W) // (2 * d)
    row = pl.BlockSpec((tm, d), lambda i: (i, 0))
    wspec = pl.BlockSpec((d, d), lambda i: (0, 0))
    return pl.pallas_call(
        _merge_kernel,
        grid=(n // tm,),
        in_specs=[row, row, row, pl.BlockSpec((tm, 2 * d), lambda i: (i, gate_blk)), wspec, wspec, wspec],
        out_specs=row,
        out_shape=jax.ShapeDtypeStruct((n, d), F32),
        compiler_params=_params("parallel"),
        name="merge",
    )(h, ya, yb, z, wa, wb, wo)


def kernel(x, meta_tokens, norm_ffn1, ffn1_w_gate, ffn1_w_up, ffn1_w_down, norm_mix, w_in, gla_w_lr, gla_b_lr,
           gla_norm, w_up_a, w_up_b, w_out, rel_bias, norm_ffn2, ffn2_w_gate, ffn2_w_up, ffn2_w_down, norm_final):
    nb, seq, d = x.shape
    assert norm_ffn1.shape[0] == 1, "single-layer block only"
    assert meta_tokens.shape[0] == N_META
    tq = 256
    assert seq % tq == 0 and (ZG_W + ZD_W) % (2 * d) == 0 and ZG_W % (DSA_HEADS * DSA_DH) == 0
    topk = min(TOPK_MAX, seq // 4)
    dff = ffn1_w_gate.shape[2]
    tf = dff // 2
    bf = lambda a: a.astype(BF16)

    sizes = (GLA_HEADS * GLA_DK, GLA_HEADS * GLA_DK, GLA_HEADS * GLA_DV, GLA_HEADS * GLA_DV, GLA_LOWRANK,
             DSA_HEADS * DSA_DH, DSA_HEADS * DSA_DH, DSA_HEADS * DSA_DH, IQ_W, IDX_DH, IDX_HEADS, d, d)
    offs = np.concatenate([[0], np.cumsum(sizes)])
    (c_gq, c_gk, c_gv, c_gr, c_lr, c_dq, c_dk, c_dv, c_iq, c_ik, c_iw, c_ga, c_gb) = [
        w_in[0][:, offs[i]:offs[i + 1]] for i in range(len(sizes))]
    c_gq = c_gq * (GLA_DK ** -0.5)
    c_dq = c_dq * (DSA_DH ** -0.5 * LOG2E)
    c_iw = c_iw * ((IDX_HEADS ** -0.5) * (IDX_DH ** -0.5))
    w_big = bf(jnp.concatenate([c_gq, c_gk, c_gv, c_gr, c_dq, c_dk, c_dv, c_ga, c_gb], axis=1))
    pad_w = SMALL_W - (IDX_DH + IDX_HEADS + GLA_LOWRANK)
    w_small = bf(jnp.concatenate([c_iq, c_ik, c_iw, c_lr, jnp.zeros((d, pad_w), F32)], axis=1))
    wlr_pad = jnp.zeros((SMALL_W, GLA_HEADS * GLA_DK), F32).at[LR_OFF:LR_OFF + GLA_LOWRANK].set(gla_w_lr[0])

    xr = x.reshape(nb * seq, d)
    g1, gm = norm_ffn1, norm_mix
    wg1, wu1, wd1 = bf(ffn1_w_gate[0]), bf(ffn1_w_up[0]), bf(ffn1_w_down[0])

    h1 = _ffn(xr, g1, wg1, wu1, wd1, tm=512, tf=tf)
    h1m = _ffn(meta_tokens, g1, wg1, wu1, wd1, tm=512, tf=tf)
    z = _proj(h1, gm, w_big, BF16, tm=512, tn=2048, name="proj_big")
    zi = _proj(h1, gm, w_small, F32, tm=512, tn=IQ_W + SMALL_W, name="proj_small")
    zm = _proj(h1m, gm, w_big, BF16, tm=512, tn=2048, name="proj_big_meta")
    zim = _proj(h1m, gm, w_small, F32, tm=512, tn=IQ_W + SMALL_W, name="proj_small_meta")

    z3 = z.reshape(nb, seq, z.shape[1])
    zi3 = zi.reshape(nb, seq, zi.shape[1])

    ya = _gla(z3, zi3, zm, zim, wlr_pad, gla_b_lr, gla_norm, chunk=128)

    hd = DSA_HEADS * DSA_DH
    n_chunks = seq // tq + 1
    meta_chunk = lambda a: jnp.pad(a, ((0, tq - N_META), (0, 0)))
    with_meta = lambda m, xs: jnp.concatenate([jnp.broadcast_to(meta_chunk(m)[None], (nb, tq, m.shape[1])), xs], 1)
    k_meta = meta_chunk(zm[:, ZG_W + hd:ZG_W + 2 * hd])
    v_all = with_meta(zm[:, ZG_W + 2 * hd:ZG_W + 3 * hd], z3[:, :, ZG_W + 2 * hd:ZG_W + 3 * hd])
    vt5 = v_all.reshape(nb, n_chunks, tq, DSA_HEADS, DSA_DH).transpose(0, 1, 3, 4, 2)
    vt = jnp.concatenate([vt5, jnp.ones((nb, n_chunks, DSA_HEADS, BF16_SUBLANES, tq), BF16)], axis=3)
    vt = vt.reshape(nb, n_chunks, DSA_HEADS * VROWS, tq)
    ik_all = with_meta(zim[:, IQ_W + IK_OFF:IQ_W + IK_OFF + IDX_DH], zi3[:, :, IQ_W + IK_OFF:IQ_W + IK_OFF + IDX_DH])
    ik = bf(ik_all.reshape(nb, n_chunks, tq, IDX_DH))
    yb = _dsa(z3, vt, zi3, k_meta, ik, rel_bias, tq=tq, topk=topk)

    h2 = _merge(h1, ya.reshape(nb * seq, -1), yb.reshape(nb * seq, -1), z,
                bf(w_up_a[0]), bf(w_up_b[0]), bf(w_out[0]), tm=512)
    out = _ffn(h2, norm_ffn2, bf(ffn2_w_gate[0]), bf(ffn2_w_up[0]), bf(ffn2_w_down[0]), norm_final[None, :],
               tm=512, tf=tf)
    return out.reshape(nb, seq, d)
```

```python
import functools

import numpy as np
import jax
import jax.numpy as jnp
from jax import lax
from jax.experimental import pallas as pl
from jax.experimental.pallas import tpu as pltpu

F32 = jnp.float32
BF16 = jnp.bfloat16
I32 = jnp.int32
HIGHEST = lax.Precision.HIGHEST

N_META = 16
EPS = 1e-6
GLA_HEADS, GLA_DK, GLA_DV, GLA_LOWRANK, GLA_TAU = 4, 128, 256, 16, 16.0
DSA_HEADS, DSA_DH, IDX_HEADS, IDX_DH, TOPK_MAX = 8, 128, 8, 64, 256
REL_BUCKETS, REL_MAX_DIST = 32, 128

LANES = 128
SUBLANES = 8
VMEM_LIMIT = 56 * 1024 * 1024
INT_MIN = -(2 ** 31)
LOG2E = float(np.log2(np.e))
BF16_SUBLANES = 16
VROWS = DSA_DH + BF16_SUBLANES
KV_SLOTS = 8
KV_AHEAD = 5
assert KV_AHEAD + 2 <= KV_SLOTS
DSA_LOGIT_BUFFERS = 8

ZG_W = 2 * GLA_HEADS * GLA_DK + 2 * GLA_HEADS * GLA_DV
ZD_W = 2 * DSA_HEADS * DSA_DH
SMALL_W = LANES
IQ_W = IDX_HEADS * IDX_DH
IK_OFF, IW_OFF, LR_OFF = 0, IDX_DH, IDX_DH + IDX_HEADS


def _rel_bucket_thresholds():
    n = np.arange(0, 4 * REL_MAX_DIST)
    max_exact = REL_BUCKETS // 2
    nf = np.maximum(n, max_exact).astype(np.float32)
    large = max_exact + (np.log(nf / np.float32(max_exact)) / np.float32(np.log(REL_MAX_DIST / max_exact))
                         * np.float32(REL_BUCKETS - max_exact)).astype(np.int32)
    large64 = max_exact + (np.log(np.maximum(n, max_exact) / max_exact) / np.log(REL_MAX_DIST / max_exact)
                           * (REL_BUCKETS - max_exact)).astype(np.int64)
    assert (large == large64).all()
    bucket = np.where(n < max_exact, n, np.minimum(large, REL_BUCKETS - 1))
    assert (np.diff(bucket) >= 0).all()
    return [int(np.argmax(bucket >= b)) for b in range(REL_BUCKETS)]


BUCKET_START = _rel_bucket_thresholds()


def _rms(x, g):
    return x * lax.rsqrt(jnp.mean(x * x, axis=-1, keepdims=True) + EPS) * g


def _sigmoid(x):
    return 1.0 / (1.0 + jnp.exp(-x))


def _dot_nt(a, b):
    return lax.dot_general(a, b, (((1,), (1,)), ((), ())), preferred_element_type=F32)


def _params(*sem):
    return pltpu.CompilerParams(dimension_semantics=sem, vmem_limit_bytes=VMEM_LIMIT)


def _ffn_kernel(*refs, final_norm):
    if final_norm:
        x_ref, g_ref, wg_ref, wu_ref, wd_ref, gf_ref, o_ref, xn_ref = refs
    else:
        x_ref, g_ref, wg_ref, wu_ref, wd_ref, o_ref, xn_ref = refs
    j = pl.program_id(1)

    @pl.when(j == 0)
    def _():
        xn_ref[...] = _rms(x_ref[...], g_ref[...]).astype(BF16)
        o_ref[...] = jnp.zeros_like(o_ref)

    xn = xn_ref[...]
    a = jnp.dot(xn, wg_ref[...], preferred_element_type=F32)
    b = jnp.dot(xn, wu_ref[...], preferred_element_type=F32)
    hmid = (a * _sigmoid(a) * b).astype(BF16)
    o_ref[...] += jnp.dot(hmid, wd_ref[...], preferred_element_type=F32)

    @pl.when(j == pl.num_programs(1) - 1)
    def _():
        h = x_ref[...] + 0.5 * o_ref[...]
        if final_norm:
            h = _rms(h, gf_ref[...])
        o_ref[...] = h


def _ffn(x, g, wg, wu, wd, gf=None, *, tm, tf):
    n, d = x.shape
    dff = wg.shape[1]
    tm = min(tm, n)
    row = pl.BlockSpec((tm, d), lambda i, j: (i, 0))
    vec = pl.BlockSpec((1, d), lambda i, j: (0, 0))
    in_specs = [row, vec,
                pl.BlockSpec((d, tf), lambda i, j: (0, j)),
                pl.BlockSpec((d, tf), lambda i, j: (0, j)),
                pl.BlockSpec((tf, d), lambda i, j: (j, 0))]
    args = [x, g, wg, wu, wd]
    if gf is not None:
        in_specs.append(vec)
        args.append(gf)
    return pl.pallas_call(
        functools.partial(_ffn_kernel, final_norm=gf is not None),
        grid=(n // tm, dff // tf),
        in_specs=in_specs,
        out_specs=row,
        out_shape=jax.ShapeDtypeStruct((n, d), F32),
        scratch_shapes=[pltpu.VMEM((tm, d), BF16)],
        compiler_params=_params("parallel", "arbitrary"),
        name="ffn_final" if gf is not None else "ffn",
    )(*args)


def _proj_kernel(x_ref, g_ref, w_ref, ws_ref, o_ref, os_ref, xn_ref):
    @pl.when(pl.program_id(1) == 0)
    def _():
        xn = _rms(x_ref[...], g_ref[...]).astype(BF16)
        xn_ref[...] = xn
        os_ref[...] = jnp.dot(xn, ws_ref[...], preferred_element_type=F32)

    o_ref[...] = jnp.dot(xn_ref[...], w_ref[...], preferred_element_type=F32).astype(o_ref.dtype)


def _proj(x, g, w, ws, *, tm, tn, name):
    n, d = x.shape
    width, small = w.shape[1], ws.shape[1]
    tm = min(tm, n)
    return pl.pallas_call(
        _proj_kernel,
        grid=(n // tm, width // tn),
        in_specs=[pl.BlockSpec((tm, d), lambda i, j: (i, 0)),
                  pl.BlockSpec((1, d), lambda i, j: (0, 0)),
                  pl.BlockSpec((d, tn), lambda i, j: (0, j)),
                  pl.BlockSpec((d, small), lambda i, j: (0, 0))],
        out_specs=[pl.BlockSpec((tm, tn), lambda i, j: (i, j)),
                   pl.BlockSpec((tm, small), lambda i, j: (i, 0))],
        out_shape=[jax.ShapeDtypeStruct((n, width), BF16), jax.ShapeDtypeStruct((n, small), F32)],
        scratch_shapes=[pltpu.VMEM((tm, d), BF16)],
        compiler_params=_params("parallel", "arbitrary"),
        name=name,
    )(x, g, w, ws)


def _values_t_kernel(x_ref, xm_ref, g_ref, w_ref, o_ref):
    def emit(x):
        xn = _rms(x, g_ref[...]).astype(BF16)
        vt = _dot_nt(w_ref[...], xn).astype(o_ref.dtype)
        ones = jnp.ones((BF16_SUBLANES, vt.shape[1]), o_ref.dtype)
        for h in range(DSA_HEADS):
            o_ref[0, 0, h * VROWS:h * VROWS + DSA_DH, :] = vt[h * DSA_DH:(h + 1) * DSA_DH]
            o_ref[0, 0, h * VROWS + DSA_DH:(h + 1) * VROWS, :] = ones

    pl.when(pl.program_id(1) == 0)(lambda: emit(xm_ref[...]))
    pl.when(pl.program_id(1) > 0)(lambda: emit(x_ref[...]))


def _values_t(x, x_meta, g, w_t, *, nb, tq):
    n, d = x.shape
    nq = n // nb // tq
    return pl.pallas_call(
        _values_t_kernel,
        grid=(nb, nq + 1),
        in_specs=[pl.BlockSpec((tq, d), lambda b, c: (b * nq + jnp.maximum(c - 1, 0), 0)),
                  pl.BlockSpec((tq, d), lambda b, c: (0, 0)),
                  pl.BlockSpec((1, d), lambda b, c: (0, 0)),
                  pl.BlockSpec(w_t.shape, lambda b, c: (0, 0))],
        out_specs=pl.BlockSpec((1, 1, DSA_HEADS * VROWS, tq), lambda b, c: (b, c, 0, 0)),
        out_shape=jax.ShapeDtypeStruct((nb, nq + 1, DSA_HEADS * VROWS, tq), BF16),
        compiler_params=_params("parallel", "arbitrary"),
        name="values_t",
    )(x, x_meta, g, w_t)


def _lower_tri(n):
    r = lax.broadcasted_iota(I32, (n, n), 0)
    c = lax.broadcasted_iota(I32, (n, n), 1)
    return c <= r


def _gla_kernel(zg_ref, zs_ref, zgm_ref, zsm_ref, wlr_ref, blr_ref, gn_ref, ya_ref, st_ref, *, nb, chunk):
    H, dk, dv = GLA_HEADS, GLA_DK, GLA_DV
    kw = H * dk
    wlr = wlr_ref[...]
    blr = blr_ref[...]
    gn = gn_ref[...]

    def log_decay(zs):
        x = jnp.dot(zs, wlr, preferred_element_type=F32, precision=HIGHEST) + blr
        return (jnp.minimum(x, 0.0) - jnp.log(1.0 + jnp.exp(-jnp.abs(x)))) * (1.0 / GLA_TAU)

    def cumsum_rows(g):
        tri = _lower_tri(g.shape[0]).astype(F32)
        return jnp.dot(tri, g, preferred_element_type=F32, precision=HIGHEST)

    @pl.when(pl.program_id(0) == 0)
    def _():
        bc = cumsum_rows(log_decay(zsm_ref[...]))
        bl = bc[N_META - 1:N_META, :]
        zgm = zgm_ref[...]
        kl = (zgm[:, kw:2 * kw].astype(F32) * jnp.exp(bl - bc)).astype(BF16)
        v = zgm[:, 2 * kw:2 * kw + H * dv]
        for h in range(H):
            st = lax.dot_general(v[:, h * dv:(h + 1) * dv], kl[:, h * dk:(h + 1) * dk],
                                 (((0,), (0,)), ((), ())), preferred_element_type=F32)
            for b in range(nb):
                st_ref[b * H + h] = st

    causal = _lower_tri(chunk)
    mid = chunk // 2
    for b in range(nb):
        zg = zg_ref[b]
        bc = cumsum_rows(log_decay(zs_ref[b]))
        bl = bc[chunk - 1:chunk, :]
        bm = bc[mid - 1:mid, :]
        qe = zg[:, :kw].astype(F32) * jnp.exp(bc - bm)
        ke = zg[:, kw:2 * kw].astype(F32) * jnp.exp(bm - bc)
        qs = (qe * jnp.exp(bm)).astype(BF16)
        kl = (ke * jnp.exp(bl - bm)).astype(BF16)
        qe = qe.astype(BF16)
        ke = ke.astype(BF16)
        dec = jnp.exp(bl)
        for h in range(H):
            ks = slice(h * dk, (h + 1) * dk)
            v = zg[:, 2 * kw + h * dv:2 * kw + (h + 1) * dv]
            r = zg[:, 2 * kw + H * dv + h * dv:2 * kw + H * dv + (h + 1) * dv].astype(F32)
            a = jnp.where(causal, _dot_nt(qe[:, ks], ke[:, ks]), 0.0).astype(BF16)
            st = st_ref[b * H + h]
            o = jnp.dot(a, v, preferred_element_type=F32) + _dot_nt(qs[:, ks], st.astype(BF16))
            st_ref[b * H + h] = st * dec[:, ks] + lax.dot_general(
                v, kl[:, ks], (((0,), (0,)), ((), ())), preferred_element_type=F32)
            y = _rms(o, gn) * (r * _sigmoid(r))
            ya_ref[b, :, h * dv:(h + 1) * dv] = y.astype(ya_ref.dtype)


def _gla(z3, zi3, z_meta, zi_meta, wlr_pad, blr, gn, *, chunk):
    nb, seq, _ = z3.shape
    H, dk, dv = GLA_HEADS, GLA_DK, GLA_DV
    small_blk = IQ_W // SMALL_W
    return pl.pallas_call(
        functools.partial(_gla_kernel, nb=nb, chunk=chunk),
        grid=(seq // chunk,),
        in_specs=[pl.BlockSpec((nb, chunk, ZG_W), lambda c: (0, c, 0)),
                  pl.BlockSpec((nb, chunk, SMALL_W), lambda c: (0, c, small_blk)),
                  pl.BlockSpec((N_META, ZG_W), lambda c: (0, 0)),
                  pl.BlockSpec((N_META, SMALL_W), lambda c: (0, small_blk)),
                  pl.BlockSpec((SMALL_W, H * dk), lambda c: (0, 0)),
                  pl.BlockSpec((1, H * dk), lambda c: (0, 0)),
                  pl.BlockSpec((1, dv), lambda c: (0, 0))],
        out_specs=pl.BlockSpec((nb, chunk, H * dv), lambda c: (0, c, 0)),
        out_shape=jax.ShapeDtypeStruct((nb, seq, H * dv), BF16),
        scratch_shapes=[pltpu.VMEM((nb * H, dv, dk), F32)],
        compiler_params=_params("arbitrary"),
        name="gla",
    )(z3, zi3, z_meta, zi_meta, wlr_pad, blr, gn)


def _dsa_kernel(q_ref, zi_ref, ik_ref, relb_ref, km_hbm, z_hbm, vt_hbm, yb_ref,
                sc_ref, thr_ref, neg_ref, s_ref, cm_ref, m_ref, acc_ref, bias_ref,
                kbuf_ref, vbuf_ref, sem_ref, *, tq, topk):
    H, dh = DSA_HEADS, DSA_DH
    hd = H * dh
    b_id, qi = pl.program_id(0), pl.program_id(1)
    n_chunks = qi + 2
    krow = lax.broadcasted_iota(I32, (tq, tq), 0)
    qcol = lax.broadcasted_iota(I32, (tq, tq), 1)

    last_chunk = qi + 1

    def k_copy(c):
        slot = c % KV_SLOTS
        if isinstance(c, int) and c == 0:
            src = km_hbm
        else:
            start = (c - 1) * tq
            rows = pl.ds(start if isinstance(start, int) else pl.multiple_of(start, tq), tq)
            src = z_hbm.at[b_id, rows, pl.ds(ZG_W + hd, hd)]
        return pltpu.make_async_copy(src, kbuf_ref.at[slot], sem_ref.at[0, slot])

    def v_copy(c):
        slot = c % KV_SLOTS
        return pltpu.make_async_copy(vt_hbm.at[b_id, c], vbuf_ref.at[slot], sem_ref.at[1, slot])

    def kv_fetch(c):
        k_copy(c).start()
        v_copy(c).start()

    kv_fetch(0)
    kv_fetch(1)
    for c in range(2, min(KV_AHEAD, ik_ref.shape[1] - 1) + 1):
        pl.when(c <= last_chunk)(functools.partial(kv_fetch, c))

    @pl.when((b_id == 0) & (qi == 0))
    def _():
        for t, dist in enumerate((N_META + qcol - krow, qcol - krow, tq + qcol - krow)):
            for h in range(H):
                far = relb_ref[REL_BUCKETS - 1, h]
                tile = jnp.full((tq, tq), (relb_ref[0, h] - far) * LOG2E, F32)
                for bkt in range(1, REL_BUCKETS):
                    tile = jnp.where(dist >= BUCKET_START[bkt], (relb_ref[bkt, h] - far) * LOG2E, tile)
                bias_ref[t, h] = tile

    def score_and_select():
        zi = zi_ref[0]
        m_ref[...] = jnp.full(m_ref.shape, -1e30, F32)
        acc_ref[...] = jnp.zeros(acc_ref.shape, F32)

        iq = [zi[:, h * IDX_DH:(h + 1) * IDX_DH].astype(BF16) for h in range(IDX_HEADS)]
        small_t = zi[:, IQ_W:].T
        iw = [small_t[IW_OFF + h:IW_OFF + h + 1, :] for h in range(IDX_HEADS)]

        def scores(ci):
            ikc = ik_ref[0, ci]
            sc = jnp.zeros((tq, tq), F32)
            for h in range(IDX_HEADS):
                sc += iw[h] * jnp.maximum(_dot_nt(ikc, iq[h]), 0.0)
            return sc

        sc_ref[0] = jnp.where(krow < N_META, scores(0), -jnp.inf)

        def full_chunk(ci, carry):
            sc_ref[ci] = scores(ci)
            return carry

        lax.fori_loop(1, qi + 1, full_chunk, 0)
        sc_ref[qi + 1] = jnp.where(krow <= qcol, scores(qi + 1), -jnp.inf)

        def as_float(key):
            return pltpu.bitcast(key ^ (lax.shift_right_arithmetic(key, 31) & 0x7FFFFFFF), F32)

        def count(cand, strict=False):
            cand8 = jnp.broadcast_to(cand, (SUBLANES, tq))

            def body(ci, accs):
                accs = list(accs)
                for j in range(tq // SUBLANES):
                    blk = sc_ref[ci, j * SUBLANES:(j + 1) * SUBLANES, :]
                    hit = blk > cand8 if strict else blk >= cand8
                    accs[j % len(accs)] = accs[j % len(accs)] + jnp.where(hit, 1, 0)
                return tuple(accs)

            accs = lax.fori_loop(0, n_chunks, body, tuple(jnp.zeros((SUBLANES, tq), I32) for _ in range(4)))
            return jnp.sum(sum(accs[1:], accs[0]), axis=0, keepdims=True)

        def search(p, prefix):
            cand = prefix ^ lax.shift_left(jnp.int32(1), 31 - p)
            return jnp.where(count(as_float(cand)) >= topk, cand, prefix)

        prefix = lax.fori_loop(0, 32, search, jnp.full((1, tq), INT_MIN, I32))
        thr = jnp.where(prefix == INT_MIN, jnp.finfo(F32).min, as_float(prefix))
        thr_ref[...] = thr

        n_ge = jnp.where(prefix == INT_MIN, 0, count(thr))

        @pl.when(jnp.max(n_ge) > topk)
        def _():
            need = (topk - count(thr, strict=True)).astype(F32)
            incl = (qcol <= krow).astype(BF16)

            def fix(ci, seen):
                sc = sc_ref[ci]
                tie = sc == thr
                tie_f = jnp.where(tie, 1.0, 0.0)
                run = jnp.dot(incl, tie_f.astype(BF16), preferred_element_type=F32)
                drop = tie & (seen + run > need)
                sc_ref[ci] = jnp.where(drop, -jnp.inf, sc)
                return seen + jnp.sum(tie_f, axis=0, keepdims=True)

            lax.fori_loop(0, n_chunks, fix, jnp.zeros((1, tq), F32))


    def logits_stage(c, par, h, bias_idx):
        hs = slice(h * dh, (h + 1) * dh)
        s = _dot_nt(kbuf_ref[c % KV_SLOTS, :, hs], q_ref[0, :, hs]) + neg_ref[...]
        if bias_idx is not None:
            s += bias_ref[bias_idx, h]
        s_ref[par, h] = s
        cm_ref[par, h] = jnp.max(s, axis=0, keepdims=True)

    def value_stage(c, par, h):
        m_old = m_ref[h]
        m_new = jnp.maximum(m_old, cm_ref[par, h])
        p = jnp.exp2(s_ref[par, h] - m_new).astype(BF16)
        pv = jnp.dot(vbuf_ref[c % KV_SLOTS, h * VROWS:(h + 1) * VROWS, :], p, preferred_element_type=F32)
        acc_ref[h] = jnp.exp2(m_old - m_new) * acc_ref[h] + pv
        m_ref[h] = m_new

    def pipeline_step(c, par, bias_idx, first=False):
        k_copy(c).wait()
        if not first:
            v_copy(c - 1).wait()
            pl.when(c + KV_AHEAD <= last_chunk)(functools.partial(kv_fetch, c + KV_AHEAD))
        neg_ref[...] = jnp.where(sc_ref[c] >= thr_ref[...], 0.0, -jnp.inf)
        for h in range(H):
            logits_stage(c, par, h, bias_idx)
            if not first:
                value_stage(c - 1, 1 - par, h)

    def drain(c, par):
        v_copy(c).wait()
        for h in range(H):
            value_stage(c, par, h)

    score_and_select()

    @pl.when(qi == 0)
    def _():
        pipeline_step(0, 0, 0, first=True)

    @pl.when(qi > 0)
    def _():
        pipeline_step(0, 0, None, first=True)

    def far_pair(i, carry):
        pipeline_step(2 * i + 1, 1, None)
        pipeline_step(2 * i + 2, 0, None)
        return carry

    lax.fori_loop(0, (qi - 1) // 2, far_pair, 0)
    qi_odd = qi % 2 == 1

    @pl.when((qi >= 2) & ~qi_odd)
    def _():
        pipeline_step(qi - 1, 1, None)

    @pl.when(qi_odd)
    def _():
        pipeline_step(qi, 1, 2)
        pipeline_step(qi + 1, 0, 1)
        drain(qi + 1, 0)

    @pl.when((qi >= 2) & ~qi_odd)
    def _():
        pipeline_step(qi, 0, 2)

    @pl.when(~qi_odd)
    def _():
        pipeline_step(qi + 1, 1, 1)
        drain(qi + 1, 1)

    for h in range(H):
        out_t = acc_ref[h, :dh, :] / acc_ref[h, dh:dh + 1, :]
        yb_ref[0, :, h * dh:(h + 1) * dh] = out_t.T.astype(yb_ref.dtype)


def _dsa(z3, vt, zi3, k_meta, ik, rel_bias, *, tq, topk):
    nb, seq, _ = z3.shape
    H, dh = DSA_HEADS, DSA_DH
    nq = seq // tq
    n_chunks = nq + 1
    hd = H * dh
    q_blk = ZG_W // hd
    hbm = pl.BlockSpec(memory_space=pl.ANY)
    return pl.pallas_call(
        functools.partial(_dsa_kernel, tq=tq, topk=topk),
        grid=(nb, nq),
        in_specs=[pl.BlockSpec((1, tq, hd), lambda b, i: (b, i, q_blk)),
                  pl.BlockSpec((1, tq, IQ_W + SMALL_W), lambda b, i: (b, i, 0)),
                  pl.BlockSpec((1, n_chunks, tq, IDX_DH), lambda b, i: (b, 0, 0, 0)),
                  pl.BlockSpec(memory_space=pltpu.SMEM),
                  hbm, hbm, hbm],
        out_specs=pl.BlockSpec((1, tq, hd), lambda b, i: (b, i, 0)),
        out_shape=jax.ShapeDtypeStruct((nb, seq, hd), BF16),
        scratch_shapes=[pltpu.VMEM((n_chunks, tq, tq), F32),
                        pltpu.VMEM((1, tq), F32),
                        pltpu.VMEM((tq, tq), F32),
                        pltpu.VMEM((2, H, tq, tq), F32),
                        pltpu.VMEM((2, H, 1, tq), F32),
                        pltpu.VMEM((H, 1, tq), F32),
                        pltpu.VMEM((H, VROWS, tq), F32),
                        pltpu.VMEM((3, H, tq, tq), F32),
                        pltpu.VMEM((KV_SLOTS, tq, hd), BF16),
                        pltpu.VMEM((KV_SLOTS, H * VROWS, tq), BF16),
                        pltpu.SemaphoreType.DMA((2, KV_SLOTS))],
        compiler_params=_params("arbitrary", "arbitrary"),
        name="dsa",
    )(z3, zi3, ik, rel_bias, k_meta, z3, vt)


def _merge_kernel(h_ref, ya_ref, yb_ref, ga_ref, gb_ref, wa_ref, wb_ref, wo_ref, o_ref):
    ua = jnp.dot(ya_ref[...], wa_ref[...], preferred_element_type=F32)
    ub = jnp.dot(yb_ref[...], wb_ref[...], preferred_element_type=F32)
    merged = _sigmoid(ga_ref[...].astype(F32)) * ua + _sigmoid(gb_ref[...].astype(F32)) * ub
    o_ref[...] = h_ref[...] + jnp.dot(merged.astype(BF16), wo_ref[...], preferred_element_type=F32)


def _merge(h, ya, yb, z, wa, wb, wo, *, tm):
    n, d = h.shape
    gate_blk = (ZG_W + ZD_W) // d
    row = pl.BlockSpec((tm, d), lambda i: (i, 0))
    wspec = pl.BlockSpec((d, d), lambda i: (0, 0))
    return pl.pallas_call(
        _merge_kernel,
        grid=(n // tm,),
        in_specs=[row, row, row,
                  pl.BlockSpec((tm, d), lambda i: (i, gate_blk)),
                  pl.BlockSpec((tm, d), lambda i: (i, gate_blk + 1)),
                  wspec, wspec, wspec],
        out_specs=row,
        out_shape=jax.ShapeDtypeStruct((n, d), F32),
        compiler_params=_params("parallel"),
        name="merge",
    )(h, ya, yb, z, z, wa, wb, wo)


def kernel(x, meta_tokens, norm_ffn1, ffn1_w_gate, ffn1_w_up, ffn1_w_down, norm_mix, w_in, gla_w_lr, gla_b_lr,
           gla_norm, w_up_a, w_up_b, w_out, rel_bias, norm_ffn2, ffn2_w_gate, ffn2_w_up, ffn2_w_down, norm_final):
    nb, seq, d = x.shape
    assert norm_ffn1.shape[0] == 1, "single-layer block only"
    assert meta_tokens.shape[0] == N_META
    tq = 256
    assert seq % tq == 0 and (ZG_W + ZD_W) % d == 0 and ZG_W % (DSA_HEADS * DSA_DH) == 0
    topk = min(TOPK_MAX, seq // 4)
    dff = ffn1_w_gate.shape[2]
    tf = dff // 2
    bf = lambda a: a.astype(BF16)

    sizes = (GLA_HEADS * GLA_DK, GLA_HEADS * GLA_DK, GLA_HEADS * GLA_DV, GLA_HEADS * GLA_DV, GLA_LOWRANK,
             DSA_HEADS * DSA_DH, DSA_HEADS * DSA_DH, DSA_HEADS * DSA_DH, IQ_W, IDX_DH, IDX_HEADS, d, d)
    offs = np.concatenate([[0], np.cumsum(sizes)])
    (c_gq, c_gk, c_gv, c_gr, c_lr, c_dq, c_dk, c_dv, c_iq, c_ik, c_iw, c_ga, c_gb) = [
        w_in[0][:, offs[i]:offs[i + 1]] for i in range(len(sizes))]
    c_gq = c_gq * (GLA_DK ** -0.5)
    c_dq = c_dq * (DSA_DH ** -0.5 * LOG2E)
    c_iw = c_iw * ((IDX_HEADS ** -0.5) * (IDX_DH ** -0.5))
    w_big = bf(jnp.concatenate([c_gq, c_gk, c_gv, c_gr, c_dq, c_dk, c_ga, c_gb], axis=1))
    w_dv_t = bf(c_dv.T)
    pad_w = SMALL_W - (IDX_DH + IDX_HEADS + GLA_LOWRANK)
    w_small = bf(jnp.concatenate([c_iq, c_ik, c_iw, c_lr, jnp.zeros((d, pad_w), F32)], axis=1))
    wlr_pad = jnp.zeros((SMALL_W, GLA_HEADS * GLA_DK), F32).at[LR_OFF:LR_OFF + GLA_LOWRANK].set(gla_w_lr[0])

    xr = x.reshape(nb * seq, d)
    g1, gm = norm_ffn1, norm_mix
    wg1, wu1, wd1 = bf(ffn1_w_gate[0]), bf(ffn1_w_up[0]), bf(ffn1_w_down[0])

    h1 = _ffn(xr, g1, wg1, wu1, wd1, tm=1024, tf=tf)
    h1m = _ffn(meta_tokens, g1, wg1, wu1, wd1, tm=512, tf=tf)
    tn = w_big.shape[1] // 4
    z, zi = _proj(h1, gm, w_big, w_small, tm=1024, tn=tn, name="proj")
    zm, zim = _proj(h1m, gm, w_big, w_small, tm=1024, tn=tn, name="proj_meta")

    z3 = z.reshape(nb, seq, z.shape[1])
    zi3 = zi.reshape(nb, seq, zi.shape[1])

    ya = _gla(z3, zi3, zm, zim, wlr_pad, gla_b_lr, gla_norm, chunk=256)

    hd = DSA_HEADS * DSA_DH
    n_chunks = seq // tq + 1
    meta_chunk = lambda a: jnp.pad(a, ((0, tq - N_META), (0, 0)))
    with_meta = lambda m, xs: jnp.concatenate([jnp.broadcast_to(meta_chunk(m)[None], (nb, tq, m.shape[1])), xs], 1)
    k_meta = meta_chunk(zm[:, ZG_W + hd:ZG_W + 2 * hd])
    vt = _values_t(h1, meta_chunk(h1m), gm, w_dv_t, nb=nb, tq=tq)
    ik_all = with_meta(zim[:, IQ_W + IK_OFF:IQ_W + IK_OFF + IDX_DH], zi3[:, :, IQ_W + IK_OFF:IQ_W + IK_OFF + IDX_DH])
    ik = bf(ik_all.reshape(nb, n_chunks, tq, IDX_DH))
    yb = _dsa(z3, vt, zi3, k_meta, ik, rel_bias, tq=tq, topk=topk)

    h2 = _merge(h1, ya.reshape(nb * seq, -1), yb.reshape(nb * seq, -1), z,
                bf(w_up_a[0]), bf(w_up_b[0]), bf(w_out[0]), tm=512)
    out = _ffn(h2, norm_ffn2, bf(ffn2_w_gate[0]), bf(ffn2_w_up[0]), bf(ffn2_w_down[0]), norm_final[None, :],
               tm=1024, tf=tf)
    return out.reshape(nb, seq, d)
```

```python
import functools

import numpy as np
import jax
import jax.numpy as jnp
from jax import lax
from jax.experimental import pallas as pl
from jax.experimental.pallas import tpu as pltpu

F32 = jnp.float32
BF16 = jnp.bfloat16
I32 = jnp.int32
HIGHEST = lax.Precision.HIGHEST

N_META = 16
EPS = 1e-6
GLA_HEADS, GLA_DK, GLA_DV, GLA_LOWRANK, GLA_TAU = 4, 128, 256, 16, 16.0
DSA_HEADS, DSA_DH, IDX_HEADS, IDX_DH, TOPK_MAX = 8, 128, 8, 64, 256
REL_BUCKETS, REL_MAX_DIST = 32, 128

LANES = 128
SUBLANES = 8
VMEM_LIMIT = 56 * 1024 * 1024
INT_MIN = -(2 ** 31)
LOG2E = float(np.log2(np.e))
BF16_SUBLANES = 16
BF16_BITS = 16
FINE_BITS = 18
ONE16 = np.int16(1)
ZERO16 = np.int16(0)
VROWS = DSA_DH + BF16_SUBLANES
KV_SLOTS = 8
KV_AHEAD = 5
assert KV_AHEAD + 2 <= KV_SLOTS
DSA_LOGIT_BUFFERS = 8

ZG_W = 2 * GLA_HEADS * GLA_DK + 2 * GLA_HEADS * GLA_DV
ZD_W = 2 * DSA_HEADS * DSA_DH
SMALL_W = LANES
IQ_W = IDX_HEADS * IDX_DH
IK_OFF, IW_OFF, LR_OFF = 0, IDX_DH, IDX_DH + IDX_HEADS


def _rel_bucket_thresholds():
    n = np.arange(0, 4 * REL_MAX_DIST)
    max_exact = REL_BUCKETS // 2
    nf = np.maximum(n, max_exact).astype(np.float32)
    large = max_exact + (np.log(nf / np.float32(max_exact)) / np.float32(np.log(REL_MAX_DIST / max_exact))
                         * np.float32(REL_BUCKETS - max_exact)).astype(np.int32)
    large64 = max_exact + (np.log(np.maximum(n, max_exact) / max_exact) / np.log(REL_MAX_DIST / max_exact)
                           * (REL_BUCKETS - max_exact)).astype(np.int64)
    assert (large == large64).all()
    bucket = np.where(n < max_exact, n, np.minimum(large, REL_BUCKETS - 1))
    assert (np.diff(bucket) >= 0).all()
    return [int(np.argmax(bucket >= b)) for b in range(REL_BUCKETS)]


BUCKET_START = _rel_bucket_thresholds()


def _rms(x, g):
    return x * lax.rsqrt(jnp.mean(x * x, axis=-1, keepdims=True) + EPS) * g


def _sigmoid(x):
    return 1.0 / (1.0 + jnp.exp(-x))


def _dot_nt(a, b):
    return lax.dot_general(a, b, (((1,), (1,)), ((), ())), preferred_element_type=F32)


def _params(*sem):
    return pltpu.CompilerParams(dimension_semantics=sem, vmem_limit_bytes=VMEM_LIMIT)


def _ffn_kernel(*refs, final_norm):
    if final_norm:
        x_ref, g_ref, wg_ref, wu_ref, wd_ref, gf_ref, o_ref, xn_ref = refs
    else:
        x_ref, g_ref, wg_ref, wu_ref, wd_ref, o_ref, xn_ref = refs
    j = pl.program_id(1)

    @pl.when(j == 0)
    def _():
        xn_ref[...] = _rms(x_ref[...], g_ref[...]).astype(BF16)
        o_ref[...] = jnp.zeros_like(o_ref)

    xn = xn_ref[...]
    a = jnp.dot(xn, wg_ref[...], preferred_element_type=F32)
    b = jnp.dot(xn, wu_ref[...], preferred_element_type=F32)
    hmid = (a * _sigmoid(a) * b).astype(BF16)
    o_ref[...] += jnp.dot(hmid, wd_ref[...], preferred_element_type=F32)

    @pl.when(j == pl.num_programs(1) - 1)
    def _():
        h = x_ref[...] + 0.5 * o_ref[...]
        if final_norm:
            h = _rms(h, gf_ref[...])
        o_ref[...] = h


def _ffn(x, g, wg, wu, wd, gf=None, *, tm, tf):
    n, d = x.shape
    dff = wg.shape[1]
    tm = min(tm, n)
    row = pl.BlockSpec((tm, d), lambda i, j: (i, 0))
    vec = pl.BlockSpec((1, d), lambda i, j: (0, 0))
    in_specs = [row, vec,
                pl.BlockSpec((d, tf), lambda i, j: (0, j)),
                pl.BlockSpec((d, tf), lambda i, j: (0, j)),
                pl.BlockSpec((tf, d), lambda i, j: (j, 0))]
    args = [x, g, wg, wu, wd]
    if gf is not None:
        in_specs.append(vec)
        args.append(gf)
    return pl.pallas_call(
        functools.partial(_ffn_kernel, final_norm=gf is not None),
        grid=(n // tm, dff // tf),
        in_specs=in_specs,
        out_specs=row,
        out_shape=jax.ShapeDtypeStruct((n, d), F32),
        scratch_shapes=[pltpu.VMEM((tm, d), BF16)],
        compiler_params=_params("parallel", "arbitrary"),
        name="ffn_final" if gf is not None else "ffn",
    )(*args)


def _proj_kernel(x_ref, g_ref, w_ref, ws_ref, o_ref, os_ref, xn_ref):
    @pl.when(pl.program_id(1) == 0)
    def _():
        xn = _rms(x_ref[...], g_ref[...]).astype(BF16)
        xn_ref[...] = xn
        os_ref[...] = jnp.dot(xn, ws_ref[...], preferred_element_type=F32)

    o_ref[...] = jnp.dot(xn_ref[...], w_ref[...], preferred_element_type=F32).astype(o_ref.dtype)


def _proj(x, g, w, ws, *, tm, tn, name):
    n, d = x.shape
    width, small = w.shape[1], ws.shape[1]
    tm = min(tm, n)
    return pl.pallas_call(
        _proj_kernel,
        grid=(n // tm, width // tn),
        in_specs=[pl.BlockSpec((tm, d), lambda i, j: (i, 0)),
                  pl.BlockSpec((1, d), lambda i, j: (0, 0)),
                  pl.BlockSpec((d, tn), lambda i, j: (0, j)),
                  pl.BlockSpec((d, small), lambda i, j: (0, 0))],
        out_specs=[pl.BlockSpec((tm, tn), lambda i, j: (i, j)),
                   pl.BlockSpec((tm, small), lambda i, j: (i, 0))],
        out_shape=[jax.ShapeDtypeStruct((n, width), BF16), jax.ShapeDtypeStruct((n, small), F32)],
        scratch_shapes=[pltpu.VMEM((tm, d), BF16)],
        compiler_params=_params("parallel", "arbitrary"),
        name=name,
    )(x, g, w, ws)


def _values_t_kernel(x_ref, xm_ref, g_ref, w_ref, o_ref):
    def emit(x):
        xn = _rms(x, g_ref[...]).astype(BF16)
        vt = _dot_nt(w_ref[...], xn).astype(o_ref.dtype)
        ones = jnp.ones((BF16_SUBLANES, vt.shape[1]), o_ref.dtype)
        for h in range(DSA_HEADS):
            o_ref[0, 0, h * VROWS:h * VROWS + DSA_DH, :] = vt[h * DSA_DH:(h + 1) * DSA_DH]
            o_ref[0, 0, h * VROWS + DSA_DH:(h + 1) * VROWS, :] = ones

    pl.when(pl.program_id(1) == 0)(lambda: emit(xm_ref[...]))
    pl.when(pl.program_id(1) > 0)(lambda: emit(x_ref[...]))


def _values_t(x, x_meta, g, w_t, *, nb, tq):
    n, d = x.shape
    nq = n // nb // tq
    return pl.pallas_call(
        _values_t_kernel,
        grid=(nb, nq + 1),
        in_specs=[pl.BlockSpec((tq, d), lambda b, c: (b * nq + jnp.maximum(c - 1, 0), 0)),
                  pl.BlockSpec((tq, d), lambda b, c: (0, 0)),
                  pl.BlockSpec((1, d), lambda b, c: (0, 0)),
                  pl.BlockSpec(w_t.shape, lambda b, c: (0, 0))],
        out_specs=pl.BlockSpec((1, 1, DSA_HEADS * VROWS, tq), lambda b, c: (b, c, 0, 0)),
        out_shape=jax.ShapeDtypeStruct((nb, nq + 1, DSA_HEADS * VROWS, tq), BF16),
        compiler_params=_params("parallel", "arbitrary"),
        name="values_t",
    )(x, x_meta, g, w_t)


def _lower_tri(n):
    r = lax.broadcasted_iota(I32, (n, n), 0)
    c = lax.broadcasted_iota(I32, (n, n), 1)
    return c <= r


def _gla_kernel(zg_ref, zs_ref, zgm_ref, zsm_ref, wlr_ref, blr_ref, gn_ref, ya_ref, st_ref, *, nb, chunk):
    H, dk, dv = GLA_HEADS, GLA_DK, GLA_DV
    kw = H * dk
    wlr = wlr_ref[...]
    blr = blr_ref[...]
    gn = gn_ref[...]

    def log_decay(zs):
        x = jnp.dot(zs, wlr, preferred_element_type=F32, precision=HIGHEST) + blr
        return (jnp.minimum(x, 0.0) - jnp.log(1.0 + jnp.exp(-jnp.abs(x)))) * (1.0 / GLA_TAU)

    def cumsum_rows(g):
        tri = _lower_tri(g.shape[0]).astype(F32)
        return jnp.dot(tri, g, preferred_element_type=F32, precision=HIGHEST)

    @pl.when(pl.program_id(0) == 0)
    def _():
        bc = cumsum_rows(log_decay(zsm_ref[...]))
        bl = bc[N_META - 1:N_META, :]
        zgm = zgm_ref[...]
        kl = (zgm[:, kw:2 * kw].astype(F32) * jnp.exp(bl - bc)).astype(BF16)
        v = zgm[:, 2 * kw:2 * kw + H * dv]
        for h in range(H):
            st = lax.dot_general(v[:, h * dv:(h + 1) * dv], kl[:, h * dk:(h + 1) * dk],
                                 (((0,), (0,)), ((), ())), preferred_element_type=F32)
            for b in range(nb):
                st_ref[b * H + h] = st

    causal = _lower_tri(chunk)
    mid = chunk // 2
    for b in range(nb):
        zg = zg_ref[b]
        bc = cumsum_rows(log_decay(zs_ref[b]))
        bl = bc[chunk - 1:chunk, :]
        bm = bc[mid - 1:mid, :]
        qe = zg[:, :kw].astype(F32) * jnp.exp(bc - bm)
        ke = zg[:, kw:2 * kw].astype(F32) * jnp.exp(bm - bc)
        qs = (qe * jnp.exp(bm)).astype(BF16)
        kl = (ke * jnp.exp(bl - bm)).astype(BF16)
        qe = qe.astype(BF16)
        ke = ke.astype(BF16)
        dec = jnp.exp(bl)
        for h in range(H):
            ks = slice(h * dk, (h + 1) * dk)
            v = zg[:, 2 * kw + h * dv:2 * kw + (h + 1) * dv]
            r = zg[:, 2 * kw + H * dv + h * dv:2 * kw + H * dv + (h + 1) * dv].astype(F32)
            a = jnp.where(causal, _dot_nt(qe[:, ks], ke[:, ks]), 0.0).astype(BF16)
            st = st_ref[b * H + h]
            o = jnp.dot(a, v, preferred_element_type=F32) + _dot_nt(qs[:, ks], st.astype(BF16))
            st_ref[b * H + h] = st * dec[:, ks] + lax.dot_general(
                v, kl[:, ks], (((0,), (0,)), ((), ())), preferred_element_type=F32)
            y = _rms(o, gn) * (r * _sigmoid(r))
            ya_ref[b, :, h * dv:(h + 1) * dv] = y.astype(ya_ref.dtype)


def _gla(z3, zi3, z_meta, zi_meta, wlr_pad, blr, gn, *, chunk):
    nb, seq, _ = z3.shape
    H, dk, dv = GLA_HEADS, GLA_DK, GLA_DV
    small_blk = IQ_W // SMALL_W
    return pl.pallas_call(
        functools.partial(_gla_kernel, nb=nb, chunk=chunk),
        grid=(seq // chunk,),
        in_specs=[pl.BlockSpec((nb, chunk, ZG_W), lambda c: (0, c, 0)),
                  pl.BlockSpec((nb, chunk, SMALL_W), lambda c: (0, c, small_blk)),
                  pl.BlockSpec((N_META, ZG_W), lambda c: (0, 0)),
                  pl.BlockSpec((N_META, SMALL_W), lambda c: (0, small_blk)),
                  pl.BlockSpec((SMALL_W, H * dk), lambda c: (0, 0)),
                  pl.BlockSpec((1, H * dk), lambda c: (0, 0)),
                  pl.BlockSpec((1, dv), lambda c: (0, 0))],
        out_specs=pl.BlockSpec((nb, chunk, H * dv), lambda c: (0, c, 0)),
        out_shape=jax.ShapeDtypeStruct((nb, seq, H * dv), BF16),
        scratch_shapes=[pltpu.VMEM((nb * H, dv, dk), F32)],
        compiler_params=_params("arbitrary"),
        name="gla",
    )(z3, zi3, z_meta, zi_meta, wlr_pad, blr, gn)


def _dsa_kernel(q_ref, zi_ref, ik_ref, relb_ref, km_hbm, z_hbm, vt_hbm, yb_ref,
                sc_ref, sb_ref, thr_ref, neg_ref, s_ref, cm_ref, m_ref, acc_ref, bias_ref,
                kbuf_ref, vbuf_ref, sem_ref, *, tq, topk):
    H, dh = DSA_HEADS, DSA_DH
    hd = H * dh
    b_id, qi = pl.program_id(0), pl.program_id(1)
    n_chunks = qi + 2
    krow = lax.broadcasted_iota(I32, (tq, tq), 0)
    qcol = lax.broadcasted_iota(I32, (tq, tq), 1)

    last_chunk = qi + 1

    def k_copy(c):
        slot = c % KV_SLOTS
        if isinstance(c, int) and c == 0:
            src = km_hbm
        else:
            start = (c - 1) * tq
            rows = pl.ds(start if isinstance(start, int) else pl.multiple_of(start, tq), tq)
            src = z_hbm.at[b_id, rows, pl.ds(ZG_W + hd, hd)]
        return pltpu.make_async_copy(src, kbuf_ref.at[slot], sem_ref.at[0, slot])

    def v_copy(c):
        slot = c % KV_SLOTS
        return pltpu.make_async_copy(vt_hbm.at[b_id, c], vbuf_ref.at[slot], sem_ref.at[1, slot])

    def kv_fetch(c):
        k_copy(c).start()
        v_copy(c).start()

    kv_fetch(0)
    kv_fetch(1)
    for c in range(2, min(KV_AHEAD, ik_ref.shape[1] - 1) + 1):
        pl.when(c <= last_chunk)(functools.partial(kv_fetch, c))

    @pl.when((b_id == 0) & (qi == 0))
    def _():
        for t, dist in enumerate((N_META + qcol - krow, qcol - krow, tq + qcol - krow)):
            for h in range(H):
                far = relb_ref[REL_BUCKETS - 1, h]
                tile = jnp.full((tq, tq), (relb_ref[0, h] - far) * LOG2E, F32)
                for bkt in range(1, REL_BUCKETS):
                    tile = jnp.where(dist >= BUCKET_START[bkt], (relb_ref[bkt, h] - far) * LOG2E, tile)
                bias_ref[t, h] = tile

    def score_and_select():
        zi = zi_ref[0]
        m_ref[...] = jnp.full(m_ref.shape, -1e30, F32)
        acc_ref[...] = jnp.zeros(acc_ref.shape, F32)

        iq = [zi[:, h * IDX_DH:(h + 1) * IDX_DH].astype(BF16) for h in range(IDX_HEADS)]
        small_t = zi[:, IQ_W:].T
        iw = [small_t[IW_OFF + h:IW_OFF + h + 1, :] for h in range(IDX_HEADS)]

        def scores(ci):
            ikc = ik_ref[0, ci]
            sc = jnp.zeros((tq, tq), F32)
            for h in range(IDX_HEADS):
                sc += iw[h] * jnp.maximum(_dot_nt(ikc, iq[h]), 0.0)
            return sc

        def store_scores(ci, sc):
            sc_ref[ci] = sc
            sb_ref[ci] = sc.astype(BF16)

        store_scores(0, jnp.where(krow < N_META, scores(0), -jnp.inf))

        def full_chunk(ci, carry):
            store_scores(ci, scores(ci))
            return carry

        lax.fori_loop(1, qi + 1, full_chunk, 0)
        store_scores(qi + 1, jnp.where(krow <= qcol, scores(qi + 1), -jnp.inf))

        def as_float(key):
            return pltpu.bitcast(key ^ (lax.shift_right_arithmetic(key, 31) & 0x7FFFFFFF), F32)

        def count_rounded(cand):
            cand16 = jnp.broadcast_to(cand, (BF16_SUBLANES, tq)).astype(BF16)

            def body(ci, accs):
                accs = list(accs)
                for j in range(tq // BF16_SUBLANES):
                    blk = sb_ref[ci, j * BF16_SUBLANES:(j + 1) * BF16_SUBLANES, :]
                    accs[j % len(accs)] = accs[j % len(accs)] + jnp.where(blk >= cand16, ONE16, ZERO16)
                return tuple(accs)

            accs = lax.fori_loop(0, n_chunks, body,
                                 tuple(jnp.zeros((BF16_SUBLANES, tq), jnp.int16) for _ in range(4)))
            tot = sum((a.astype(I32) for a in accs[1:]), accs[0].astype(I32))
            return jnp.sum(tot, axis=0, keepdims=True)

        def count(cand, strict=False):
            cand8 = jnp.broadcast_to(cand, (SUBLANES, tq))

            def body(ci, accs):
                accs = list(accs)
                for j in range(tq // SUBLANES):
                    blk = sc_ref[ci, j * SUBLANES:(j + 1) * SUBLANES, :]
                    hit = blk > cand8 if strict else blk >= cand8
                    accs[j % len(accs)] = accs[j % len(accs)] + jnp.where(hit, 1, 0)
                return tuple(accs)

            accs = lax.fori_loop(0, n_chunks, body, tuple(jnp.zeros((SUBLANES, tq), I32) for _ in range(4)))
            return jnp.sum(sum(accs[1:], accs[0]), axis=0, keepdims=True)

        bias16 = 1 << (BF16_BITS - 1)

        def bf16_bits(k):
            return lax.shift_left(k ^ (lax.shift_right_arithmetic(k, BF16_BITS - 1) & (bias16 - 1)), BF16_BITS)

        def coarse(p, u):
            cand = u | lax.shift_left(jnp.int32(1), BF16_BITS - 1 - p)
            hit = count_rounded(pltpu.bitcast(bf16_bits(cand - bias16), F32)) >= topk
            return jnp.where(hit, cand, u)

        u = lax.fori_loop(0, BF16_BITS, coarse, jnp.zeros((1, tq), I32))
        none = u == 0
        low_bits = bf16_bits(u - bias16 - 1)
        low = low_bits ^ (lax.shift_right_arithmetic(low_bits, 31) & 0x7FFFFFFF)

        def fine(p, d):
            cand = d | lax.shift_left(jnp.int32(1), FINE_BITS - 1 - p)
            return jnp.where(count(as_float(low + cand)) >= topk, cand, d)

        d = lax.fori_loop(0, FINE_BITS, fine, jnp.zeros((1, tq), I32))
        thr = jnp.where(none, jnp.finfo(F32).min, as_float(low + d))
        thr_ref[...] = thr

        n_ge = jnp.where(none, 0, count(thr))

        @pl.when(jnp.max(n_ge) > topk)
        def _():
            need = (topk - count(thr, strict=True)).astype(F32)
            incl = (qcol <= krow).astype(BF16)

            def fix(ci, seen):
                sc = sc_ref[ci]
                tie = sc == thr
                tie_f = jnp.where(tie, 1.0, 0.0)
                run = jnp.dot(incl, tie_f.astype(BF16), preferred_element_type=F32)
                drop = tie & (seen + run > need)
                sc_ref[ci] = jnp.where(drop, -jnp.inf, sc)
                return seen + jnp.sum(tie_f, axis=0, keepdims=True)

            lax.fori_loop(0, n_chunks, fix, jnp.zeros((1, tq), F32))


    def logits_stage(c, par, h, bias_idx):
        hs = slice(h * dh, (h + 1) * dh)
        s = _dot_nt(kbuf_ref[c % KV_SLOTS, :, hs], q_ref[0, :, hs]) + neg_ref[...]
        if bias_idx is not None:
            s += bias_ref[bias_idx, h]
        s_ref[par, h] = s
        cm_ref[par, h] = jnp.max(s, axis=0, keepdims=True)

    def value_stage(c, par, h):
        m_old = m_ref[h]
        m_new = jnp.maximum(m_old, cm_ref[par, h])
        p = jnp.exp2(s_ref[par, h] - m_new).astype(BF16)
        pv = jnp.dot(vbuf_ref[c % KV_SLOTS, h * VROWS:(h + 1) * VROWS, :], p, preferred_element_type=F32)
        acc_ref[h] = jnp.exp2(m_old - m_new) * acc_ref[h] + pv
        m_ref[h] = m_new

    def pipeline_step(c, par, bias_idx, first=False):
        k_copy(c).wait()
        if not first:
            v_copy(c - 1).wait()
            pl.when(c + KV_AHEAD <= last_chunk)(functools.partial(kv_fetch, c + KV_AHEAD))
        neg_ref[...] = jnp.where(sc_ref[c] >= thr_ref[...], 0.0, -jnp.inf)
        for h in range(H):
            logits_stage(c, par, h, bias_idx)
            if not first:
                value_stage(c - 1, 1 - par, h)

    def drain(c, par):
        v_copy(c).wait()
        for h in range(H):
            value_stage(c, par, h)

    score_and_select()

    @pl.when(qi == 0)
    def _():
        pipeline_step(0, 0, 0, first=True)

    @pl.when(qi > 0)
    def _():
        pipeline_step(0, 0, None, first=True)

    def far_pair(i, carry):
        pipeline_step(2 * i + 1, 1, None)
        pipeline_step(2 * i + 2, 0, None)
        return carry

    lax.fori_loop(0, (qi - 1) // 2, far_pair, 0)
    qi_odd = qi % 2 == 1

    @pl.when((qi >= 2) & ~qi_odd)
    def _():
        pipeline_step(qi - 1, 1, None)

    @pl.when(qi_odd)
    def _():
        pipeline_step(qi, 1, 2)
        pipeline_step(qi + 1, 0, 1)
        drain(qi + 1, 0)

    @pl.when((qi >= 2) & ~qi_odd)
    def _():
        pipeline_step(qi, 0, 2)

    @pl.when(~qi_odd)
    def _():
        pipeline_step(qi + 1, 1, 1)
        drain(qi + 1, 1)

    for h in range(H):
        out_t = acc_ref[h, :dh, :] / acc_ref[h, dh:dh + 1, :]
        yb_ref[0, :, h * dh:(h + 1) * dh] = out_t.T.astype(yb_ref.dtype)


def _dsa(z3, vt, zi3, k_meta, ik, rel_bias, *, tq, topk):
    nb, seq, _ = z3.shape
    H, dh = DSA_HEADS, DSA_DH
    nq = seq // tq
    n_chunks = nq + 1
    hd = H * dh
    q_blk = ZG_W // hd
    hbm = pl.BlockSpec(memory_space=pl.ANY)
    return pl.pallas_call(
        functools.partial(_dsa_kernel, tq=tq, topk=topk),
        grid=(nb, nq),
        in_specs=[pl.BlockSpec((1, tq, hd), lambda b, i: (b, i, q_blk)),
                  pl.BlockSpec((1, tq, IQ_W + SMALL_W), lambda b, i: (b, i, 0)),
                  pl.BlockSpec((1, n_chunks, tq, IDX_DH), lambda b, i: (b, 0, 0, 0)),
                  pl.BlockSpec(memory_space=pltpu.SMEM),
                  hbm, hbm, hbm],
        out_specs=pl.BlockSpec((1, tq, hd), lambda b, i: (b, i, 0)),
        out_shape=jax.ShapeDtypeStruct((nb, seq, hd), BF16),
        scratch_shapes=[pltpu.VMEM((n_chunks, tq, tq), F32),
                        pltpu.VMEM((n_chunks, tq, tq), BF16),
                        pltpu.VMEM((1, tq), F32),
                        pltpu.VMEM((tq, tq), F32),
                        pltpu.VMEM((2, H, tq, tq), F32),
                        pltpu.VMEM((2, H, 1, tq), F32),
                        pltpu.VMEM((H, 1, tq), F32),
                        pltpu.VMEM((H, VROWS, tq), F32),
                        pltpu.VMEM((3, H, tq, tq), F32),
                        pltpu.VMEM((KV_SLOTS, tq, hd), BF16),
                        pltpu.VMEM((KV_SLOTS, H * VROWS, tq), BF16),
                        pltpu.SemaphoreType.DMA((2, KV_SLOTS))],
        compiler_params=_params("arbitrary", "arbitrary"),
        name="dsa",
    )(z3, zi3, ik, rel_bias, k_meta, z3, vt)


def _merge_kernel(h_ref, ya_ref, yb_ref, ga_ref, gb_ref, wa_ref, wb_ref, wo_ref, o_ref):
    ua = jnp.dot(ya_ref[...], wa_ref[...], preferred_element_type=F32)
    ub = jnp.dot(yb_ref[...], wb_ref[...], preferred_element_type=F32)
    merged = _sigmoid(ga_ref[...].astype(F32)) * ua + _sigmoid(gb_ref[...].astype(F32)) * ub
    o_ref[...] = h_ref[...] + jnp.dot(merged.astype(BF16), wo_ref[...], preferred_element_type=F32)


def _merge(h, ya, yb, z, wa, wb, wo, *, tm):
    n, d = h.shape
    gate_blk = (ZG_W + ZD_W) // d
    row = pl.BlockSpec((tm, d), lambda i: (i, 0))
    wspec = pl.BlockSpec((d, d), lambda i: (0, 0))
    return pl.pallas_call(
        _merge_kernel,
        grid=(n // tm,),
        in_specs=[row, row, row,
                  pl.BlockSpec((tm, d), lambda i: (i, gate_blk)),
                  pl.BlockSpec((tm, d), lambda i: (i, gate_blk + 1)),
                  wspec, wspec, wspec],
        out_specs=row,
        out_shape=jax.ShapeDtypeStruct((n, d), F32),
        compiler_params=_params("parallel"),
        name="merge",
    )(h, ya, yb, z, z, wa, wb, wo)


def kernel(x, meta_tokens, norm_ffn1, ffn1_w_gate, ffn1_w_up, ffn1_w_down, norm_mix, w_in, gla_w_lr, gla_b_lr,
           gla_norm, w_up_a, w_up_b, w_out, rel_bias, norm_ffn2, ffn2_w_gate, ffn2_w_up, ffn2_w_down, norm_final):
    nb, seq, d = x.shape
    assert norm_ffn1.shape[0] == 1, "single-layer block only"
    assert meta_tokens.shape[0] == N_META
    tq = 256
    assert seq % tq == 0 and (ZG_W + ZD_W) % d == 0 and ZG_W % (DSA_HEADS * DSA_DH) == 0
    topk = min(TOPK_MAX, seq // 4)
    dff = ffn1_w_gate.shape[2]
    tf = dff // 2
    bf = lambda a: a.astype(BF16)

    sizes = (GLA_HEADS * GLA_DK, GLA_HEADS * GLA_DK, GLA_HEADS * GLA_DV, GLA_HEADS * GLA_DV, GLA_LOWRANK,
             DSA_HEADS * DSA_DH, DSA_HEADS * DSA_DH, DSA_HEADS * DSA_DH, IQ_W, IDX_DH, IDX_HEADS, d, d)
    offs = np.concatenate([[0], np.cumsum(sizes)])
    (c_gq, c_gk, c_gv, c_gr, c_lr, c_dq, c_dk, c_dv, c_iq, c_ik, c_iw, c_ga, c_gb) = [
        w_in[0][:, offs[i]:offs[i + 1]] for i in range(len(sizes))]
    c_gq = c_gq * (GLA_DK ** -0.5)
    c_dq = c_dq * (DSA_DH ** -0.5 * LOG2E)
    c_iw = c_iw * ((IDX_HEADS ** -0.5) * (IDX_DH ** -0.5))
    w_big = bf(jnp.concatenate([c_gq, c_gk, c_gv, c_gr, c_dq, c_dk, c_ga, c_gb], axis=1))
    w_dv_t = bf(c_dv.T)
    pad_w = SMALL_W - (IDX_DH + IDX_HEADS + GLA_LOWRANK)
    w_small = bf(jnp.concatenate([c_iq, c_ik, c_iw, c_lr, jnp.zeros((d, pad_w), F32)], axis=1))
    wlr_pad = jnp.zeros((SMALL_W, GLA_HEADS * GLA_DK), F32).at[LR_OFF:LR_OFF + GLA_LOWRANK].set(gla_w_lr[0])

    xr = x.reshape(nb * seq, d)
    g1, gm = norm_ffn1, norm_mix
    wg1, wu1, wd1 = bf(ffn1_w_gate[0]), bf(ffn1_w_up[0]), bf(ffn1_w_down[0])

    h1 = _ffn(xr, g1, wg1, wu1, wd1, tm=1024, tf=tf)
    h1m = _ffn(meta_tokens, g1, wg1, wu1, wd1, tm=512, tf=tf)
    tn = w_big.shape[1] // 4
    z, zi = _proj(h1, gm, w_big, w_small, tm=1024, tn=tn, name="proj")
    zm, zim = _proj(h1m, gm, w_big, w_small, tm=1024, tn=tn, name="proj_meta")

    z3 = z.reshape(nb, seq, z.shape[1])
    zi3 = zi.reshape(nb, seq, zi.shape[1])

    ya = _gla(z3, zi3, zm, zim, wlr_pad, gla_b_lr, gla_norm, chunk=256)

    hd = DSA_HEADS * DSA_DH
    n_chunks = seq // tq + 1
    meta_chunk = lambda a: jnp.pad(a, ((0, tq - N_META), (0, 0)))
    with_meta = lambda m, xs: jnp.concatenate([jnp.broadcast_to(meta_chunk(m)[None], (nb, tq, m.shape[1])), xs], 1)
    k_meta = meta_chunk(zm[:, ZG_W + hd:ZG_W + 2 * hd])
    vt = _values_t(h1, meta_chunk(h1m), gm, w_dv_t, nb=nb, tq=tq)
    ik_all = with_meta(zim[:, IQ_W + IK_OFF:IQ_W + IK_OFF + IDX_DH], zi3[:, :, IQ_W + IK_OFF:IQ_W + IK_OFF + IDX_DH])
    ik = bf(ik_all.reshape(nb, n_chunks, tq, IDX_DH))
    yb = _dsa(z3, vt, zi3, k_meta, ik, rel_bias, tq=tq, topk=topk)

    h2 = _merge(h1, ya.reshape(nb * seq, -1), yb.reshape(nb * seq, -1), z,
                bf(w_up_a[0]), bf(w_up_b[0]), bf(w_out[0]), tm=512)
    out = _ffn(h2, norm_ffn2, bf(ffn2_w_gate[0]), bf(ffn2_w_up[0]), bf(ffn2_w_down[0]), norm_final[None, :],
               tm=1024, tf=tf)
    return out.reshape(nb, seq, d)
```

```python
import functools

import numpy as np
import jax
import jax.numpy as jnp
from jax import lax
from jax.experimental import pallas as pl
from jax.experimental.pallas import tpu as pltpu

F32 = jnp.float32
BF16 = jnp.bfloat16
I32 = jnp.int32
HIGHEST = lax.Precision.HIGHEST

N_META = 16
EPS = 1e-6
GLA_HEADS, GLA_DK, GLA_DV, GLA_LOWRANK, GLA_TAU = 4, 128, 256, 16, 16.0
DSA_HEADS, DSA_DH, IDX_HEADS, IDX_DH, TOPK_MAX = 8, 128, 8, 64, 256
REL_BUCKETS, REL_MAX_DIST = 32, 128

LANES = 128
SUBLANES = 8
VMEM_LIMIT = 56 * 1024 * 1024
INT_MIN = -(2 ** 31)
LOG2E = float(np.log2(np.e))
BF16_SUBLANES = 16
BF16_BITS = 16
FINE_BITS = 18
ONE16 = np.int16(1)
ZERO16 = np.int16(0)
VROWS = DSA_DH + BF16_SUBLANES
KV_SLOTS = 8
KV_AHEAD = 5
assert KV_AHEAD + 2 <= KV_SLOTS
DSA_LOGIT_BUFFERS = 8

ZG_W = 2 * GLA_HEADS * GLA_DK + 2 * GLA_HEADS * GLA_DV
ZD_W = 2 * DSA_HEADS * DSA_DH
SMALL_W = LANES
IQ_W = IDX_HEADS * IDX_DH
IK_OFF, IW_OFF, LR_OFF = 0, IDX_DH, IDX_DH + IDX_HEADS


def _rel_bucket_thresholds():
    n = np.arange(0, 4 * REL_MAX_DIST)
    max_exact = REL_BUCKETS // 2
    nf = np.maximum(n, max_exact).astype(np.float32)
    large = max_exact + (np.log(nf / np.float32(max_exact)) / np.float32(np.log(REL_MAX_DIST / max_exact))
                         * np.float32(REL_BUCKETS - max_exact)).astype(np.int32)
    large64 = max_exact + (np.log(np.maximum(n, max_exact) / max_exact) / np.log(REL_MAX_DIST / max_exact)
                           * (REL_BUCKETS - max_exact)).astype(np.int64)
    assert (large == large64).all()
    bucket = np.where(n < max_exact, n, np.minimum(large, REL_BUCKETS - 1))
    assert (np.diff(bucket) >= 0).all()
    return [int(np.argmax(bucket >= b)) for b in range(REL_BUCKETS)]


BUCKET_START = _rel_bucket_thresholds()


def _rms(x, g):
    return x * lax.rsqrt(jnp.mean(x * x, axis=-1, keepdims=True) + EPS) * g


def _sigmoid(x):
    return 1.0 / (1.0 + jnp.exp(-x))


def _dot_nt(a, b):
    return lax.dot_general(a, b, (((1,), (1,)), ((), ())), preferred_element_type=F32)


def _params(*sem):
    return pltpu.CompilerParams(dimension_semantics=sem, vmem_limit_bytes=VMEM_LIMIT)


def _ffn_kernel(*refs, final_norm):
    if final_norm:
        x_ref, g_ref, wg_ref, wu_ref, wd_ref, gf_ref, o_ref, xn_ref = refs
    else:
        x_ref, g_ref, wg_ref, wu_ref, wd_ref, o_ref, xn_ref = refs
    j = pl.program_id(1)

    @pl.when(j == 0)
    def _():
        xn_ref[...] = _rms(x_ref[...], g_ref[...]).astype(BF16)
        o_ref[...] = jnp.zeros_like(o_ref)

    xn = xn_ref[...]
    a = jnp.dot(xn, wg_ref[...], preferred_element_type=F32)
    b = jnp.dot(xn, wu_ref[...], preferred_element_type=F32)
    hmid = (a * _sigmoid(a) * b).astype(BF16)
    o_ref[...] += jnp.dot(hmid, wd_ref[...], preferred_element_type=F32)

    @pl.when(j == pl.num_programs(1) - 1)
    def _():
        h = x_ref[...] + 0.5 * o_ref[...]
        if final_norm:
            h = _rms(h, gf_ref[...])
        o_ref[...] = h


def _ffn(x, g, wg, wu, wd, gf=None, *, tm, tf):
    n, d = x.shape
    dff = wg.shape[1]
    tm = min(tm, n)
    row = pl.BlockSpec((tm, d), lambda i, j: (i, 0))
    vec = pl.BlockSpec((1, d), lambda i, j: (0, 0))
    in_specs = [row, vec,
                pl.BlockSpec((d, tf), lambda i, j: (0, j)),
                pl.BlockSpec((d, tf), lambda i, j: (0, j)),
                pl.BlockSpec((tf, d), lambda i, j: (j, 0))]
    args = [x, g, wg, wu, wd]
    if gf is not None:
        in_specs.append(vec)
        args.append(gf)
    return pl.pallas_call(
        functools.partial(_ffn_kernel, final_norm=gf is not None),
        grid=(n // tm, dff // tf),
        in_specs=in_specs,
        out_specs=row,
        out_shape=jax.ShapeDtypeStruct((n, d), F32),
        scratch_shapes=[pltpu.VMEM((tm, d), BF16)],
        compiler_params=_params("parallel", "arbitrary"),
        name="ffn_final" if gf is not None else "ffn",
    )(*args)


def _proj_kernel(x_ref, g_ref, w_ref, ws_ref, o_ref, os_ref, xn_ref):
    @pl.when(pl.program_id(1) == 0)
    def _():
        xn = _rms(x_ref[...], g_ref[...]).astype(BF16)
        xn_ref[...] = xn
        os_ref[...] = jnp.dot(xn, ws_ref[...], preferred_element_type=F32)

    o_ref[...] = jnp.dot(xn_ref[...], w_ref[...], preferred_element_type=F32).astype(o_ref.dtype)


def _proj(x, g, w, ws, *, tm, tn, name):
    n, d = x.shape
    width, small = w.shape[1], ws.shape[1]
    tm = min(tm, n)
    return pl.pallas_call(
        _proj_kernel,
        grid=(n // tm, width // tn),
        in_specs=[pl.BlockSpec((tm, d), lambda i, j: (i, 0)),
                  pl.BlockSpec((1, d), lambda i, j: (0, 0)),
                  pl.BlockSpec((d, tn), lambda i, j: (0, j)),
                  pl.BlockSpec((d, small), lambda i, j: (0, 0))],
        out_specs=[pl.BlockSpec((tm, tn), lambda i, j: (i, j)),
                   pl.BlockSpec((tm, small), lambda i, j: (i, 0))],
        out_shape=[jax.ShapeDtypeStruct((n, width), BF16), jax.ShapeDtypeStruct((n, small), F32)],
        scratch_shapes=[pltpu.VMEM((tm, d), BF16)],
        compiler_params=_params("parallel", "arbitrary"),
        name=name,
    )(x, g, w, ws)


def _values_t_kernel(x_ref, xm_ref, g_ref, w_ref, o_ref):
    def emit(x):
        xn = _rms(x, g_ref[...]).astype(BF16)
        vt = _dot_nt(w_ref[...], xn).astype(o_ref.dtype)
        ones = jnp.ones((BF16_SUBLANES, vt.shape[1]), o_ref.dtype)
        for h in range(DSA_HEADS):
            o_ref[0, 0, h * VROWS:h * VROWS + DSA_DH, :] = vt[h * DSA_DH:(h + 1) * DSA_DH]
            o_ref[0, 0, h * VROWS + DSA_DH:(h + 1) * VROWS, :] = ones

    pl.when(pl.program_id(1) == 0)(lambda: emit(xm_ref[...]))
    pl.when(pl.program_id(1) > 0)(lambda: emit(x_ref[...]))


def _values_t(x, x_meta, g, w_t, *, nb, tq):
    n, d = x.shape
    nq = n // nb // tq
    return pl.pallas_call(
        _values_t_kernel,
        grid=(nb, nq + 1),
        in_specs=[pl.BlockSpec((tq, d), lambda b, c: (b * nq + jnp.maximum(c - 1, 0), 0)),
                  pl.BlockSpec((tq, d), lambda b, c: (0, 0)),
                  pl.BlockSpec((1, d), lambda b, c: (0, 0)),
                  pl.BlockSpec(w_t.shape, lambda b, c: (0, 0))],
        out_specs=pl.BlockSpec((1, 1, DSA_HEADS * VROWS, tq), lambda b, c: (b, c, 0, 0)),
        out_shape=jax.ShapeDtypeStruct((nb, nq + 1, DSA_HEADS * VROWS, tq), BF16),
        compiler_params=_params("parallel", "arbitrary"),
        name="values_t",
    )(x, x_meta, g, w_t)


def _lower_tri(n):
    r = lax.broadcasted_iota(I32, (n, n), 0)
    c = lax.broadcasted_iota(I32, (n, n), 1)
    return c <= r


def _gla_kernel(zg_ref, zs_ref, zgm_ref, zsm_ref, wlr_ref, blr_ref, gn_ref, ya_ref, st_ref, *, nb, chunk):
    H, dk, dv = GLA_HEADS, GLA_DK, GLA_DV
    kw = H * dk
    wlr = wlr_ref[...]
    blr = blr_ref[...]
    gn = gn_ref[...]

    def log_decay(zs):
        x = jnp.dot(zs, wlr, preferred_element_type=F32, precision=HIGHEST) + blr
        return (jnp.minimum(x, 0.0) - jnp.log(1.0 + jnp.exp(-jnp.abs(x)))) * (1.0 / GLA_TAU)

    def cumsum_rows(g):
        tri = _lower_tri(g.shape[0]).astype(F32)
        return jnp.dot(tri, g, preferred_element_type=F32, precision=HIGHEST)

    @pl.when(pl.program_id(0) == 0)
    def _():
        bc = cumsum_rows(log_decay(zsm_ref[...]))
        bl = bc[N_META - 1:N_META, :]
        zgm = zgm_ref[...]
        kl = (zgm[:, kw:2 * kw].astype(F32) * jnp.exp(bl - bc)).astype(BF16)
        v = zgm[:, 2 * kw:2 * kw + H * dv]
        for h in range(H):
            st = lax.dot_general(v[:, h * dv:(h + 1) * dv], kl[:, h * dk:(h + 1) * dk],
                                 (((0,), (0,)), ((), ())), preferred_element_type=F32)
            for b in range(nb):
                st_ref[b * H + h] = st

    causal = _lower_tri(chunk)
    mid = chunk // 2
    for b in range(nb):
        zg = zg_ref[b]
        bc = cumsum_rows(log_decay(zs_ref[b]))
        bl = bc[chunk - 1:chunk, :]
        bm = bc[mid - 1:mid, :]
        qe = zg[:, :kw].astype(F32) * jnp.exp(bc - bm)
        ke = zg[:, kw:2 * kw].astype(F32) * jnp.exp(bm - bc)
        qs = (qe * jnp.exp(bm)).astype(BF16)
        kl = (ke * jnp.exp(bl - bm)).astype(BF16)
        qe = qe.astype(BF16)
        ke = ke.astype(BF16)
        dec = jnp.exp(bl)
        for h in range(H):
            ks = slice(h * dk, (h + 1) * dk)
            v = zg[:, 2 * kw + h * dv:2 * kw + (h + 1) * dv]
            r = zg[:, 2 * kw + H * dv + h * dv:2 * kw + H * dv + (h + 1) * dv].astype(F32)
            a = jnp.where(causal, _dot_nt(qe[:, ks], ke[:, ks]), 0.0).astype(BF16)
            st = st_ref[b * H + h]
            o = jnp.dot(a, v, preferred_element_type=F32) + _dot_nt(qs[:, ks], st.astype(BF16))
            st_ref[b * H + h] = st * dec[:, ks] + lax.dot_general(
                v, kl[:, ks], (((0,), (0,)), ((), ())), preferred_element_type=F32)
            y = _rms(o, gn) * (r * _sigmoid(r))
            ya_ref[b, :, h * dv:(h + 1) * dv] = y.astype(ya_ref.dtype)


def _gla(z3, zi3, z_meta, zi_meta, wlr_pad, blr, gn, *, chunk):
    nb, seq, _ = z3.shape
    H, dk, dv = GLA_HEADS, GLA_DK, GLA_DV
    small_blk = IQ_W // SMALL_W
    return pl.pallas_call(
        functools.partial(_gla_kernel, nb=nb, chunk=chunk),
        grid=(seq // chunk,),
        in_specs=[pl.BlockSpec((nb, chunk, ZG_W), lambda c: (0, c, 0)),
                  pl.BlockSpec((nb, chunk, SMALL_W), lambda c: (0, c, small_blk)),
                  pl.BlockSpec((N_META, ZG_W), lambda c: (0, 0)),
                  pl.BlockSpec((N_META, SMALL_W), lambda c: (0, small_blk)),
                  pl.BlockSpec((SMALL_W, H * dk), lambda c: (0, 0)),
                  pl.BlockSpec((1, H * dk), lambda c: (0, 0)),
                  pl.BlockSpec((1, dv), lambda c: (0, 0))],
        out_specs=pl.BlockSpec((nb, chunk, H * dv), lambda c: (0, c, 0)),
        out_shape=jax.ShapeDtypeStruct((nb, seq, H * dv), BF16),
        scratch_shapes=[pltpu.VMEM((nb * H, dv, dk), F32)],
        compiler_params=_params("arbitrary"),
        name="gla",
    )(z3, zi3, z_meta, zi_meta, wlr_pad, blr, gn)


def _dsa_kernel(q_ref, zi_ref, ik_ref, relb_ref, km_hbm, z_hbm, vt_hbm, yb_ref,
                sc_ref, sb_ref, thr_ref, neg_ref, s_ref, cm_ref, m_ref, acc_ref, bias_ref,
                kbuf_ref, vbuf_ref, sem_ref, *, tq, topk):
    H, dh = DSA_HEADS, DSA_DH
    hd = H * dh
    b_id, qi = pl.program_id(0), pl.program_id(1)
    n_chunks = qi + 2
    krow = lax.broadcasted_iota(I32, (tq, tq), 0)
    qcol = lax.broadcasted_iota(I32, (tq, tq), 1)

    last_chunk = qi + 1

    def k_copy(c):
        slot = c % KV_SLOTS
        if isinstance(c, int) and c == 0:
            src = km_hbm
        else:
            start = (c - 1) * tq
            rows = pl.ds(start if isinstance(start, int) else pl.multiple_of(start, tq), tq)
            src = z_hbm.at[b_id, rows, pl.ds(ZG_W + hd, hd)]
        return pltpu.make_async_copy(src, kbuf_ref.at[slot], sem_ref.at[0, slot])

    def v_copy(c):
        slot = c % KV_SLOTS
        return pltpu.make_async_copy(vt_hbm.at[b_id, c], vbuf_ref.at[slot], sem_ref.at[1, slot])

    def kv_fetch(c):
        k_copy(c).start()
        v_copy(c).start()

    kv_fetch(0)
    kv_fetch(1)
    for c in range(2, min(KV_AHEAD, ik_ref.shape[1] - 1) + 1):
        pl.when(c <= last_chunk)(functools.partial(kv_fetch, c))

    @pl.when((b_id == 0) & (qi == 0))
    def _():
        for t, dist in enumerate((N_META + qcol - krow, qcol - krow, tq + qcol - krow)):
            for h in range(H):
                far = relb_ref[REL_BUCKETS - 1, h]
                tile = jnp.full((tq, tq), (relb_ref[0, h] - far) * LOG2E, F32)
                for bkt in range(1, REL_BUCKETS):
                    tile = jnp.where(dist >= BUCKET_START[bkt], (relb_ref[bkt, h] - far) * LOG2E, tile)
                bias_ref[t, h] = tile

    def score_and_select():
        zi = zi_ref[0]
        m_ref[...] = jnp.full(m_ref.shape, -1e30, F32)
        acc_ref[...] = jnp.zeros(acc_ref.shape, F32)

        iq = [zi[:, h * IDX_DH:(h + 1) * IDX_DH].astype(BF16) for h in range(IDX_HEADS)]
        small_t = zi[:, IQ_W:].T
        iw = [small_t[IW_OFF + h:IW_OFF + h + 1, :] for h in range(IDX_HEADS)]

        def scores(ci):
            ikc = ik_ref[0, ci]
            sc = jnp.zeros((tq, tq), F32)
            for h in range(IDX_HEADS):
                sc += iw[h] * jnp.maximum(_dot_nt(ikc, iq[h]), 0.0)
            return sc

        def store_scores(ci, sc):
            sc_ref[ci] = sc
            sb_ref[ci] = sc.astype(BF16)

        store_scores(0, jnp.where(krow < N_META, scores(0), -jnp.inf))

        def full_chunk(ci, carry):
            store_scores(ci, scores(ci))
            return carry

        lax.fori_loop(1, qi + 1, full_chunk, 0)
        store_scores(qi + 1, jnp.where(krow <= qcol, scores(qi + 1), -jnp.inf))
        store_scores(n_chunks, jnp.full((tq, tq), -jnp.inf, F32))
        n_pairs = (n_chunks + 1) // 2

        def as_float(key):
            return pltpu.bitcast(key ^ (lax.shift_right_arithmetic(key, 31) & 0x7FFFFFFF), F32)

        def count_rounded(cand):
            cand16 = jnp.broadcast_to(cand, (BF16_SUBLANES, tq)).astype(BF16)

            def body(i, accs):
                accs = list(accs)
                for ci in (2 * i, 2 * i + 1):
                    for j in range(tq // BF16_SUBLANES):
                        blk = sb_ref[ci, j * BF16_SUBLANES:(j + 1) * BF16_SUBLANES, :]
                        accs[j % len(accs)] = accs[j % len(accs)] + jnp.where(blk >= cand16, ONE16, ZERO16)
                return tuple(accs)

            accs = lax.fori_loop(0, n_pairs, body,
                                 tuple(jnp.zeros((BF16_SUBLANES, tq), jnp.int16) for _ in range(4)))
            tot = sum((a.astype(I32) for a in accs[1:]), accs[0].astype(I32))
            return jnp.sum(tot, axis=0, keepdims=True)

        def count(cand, strict=False):
            cand8 = jnp.broadcast_to(cand, (SUBLANES, tq))

            def body(i, accs):
                accs = list(accs)
                for ci in (2 * i, 2 * i + 1):
                    for j in range(tq // SUBLANES):
                        blk = sc_ref[ci, j * SUBLANES:(j + 1) * SUBLANES, :]
                        hit = blk > cand8 if strict else blk >= cand8
                        accs[j % len(accs)] = accs[j % len(accs)] + jnp.where(hit, 1, 0)
                return tuple(accs)

            accs = lax.fori_loop(0, n_pairs, body, tuple(jnp.zeros((SUBLANES, tq), I32) for _ in range(4)))
            return jnp.sum(sum(accs[1:], accs[0]), axis=0, keepdims=True)

        bias16 = 1 << (BF16_BITS - 1)

        def bf16_bits(k):
            return lax.shift_left(k ^ (lax.shift_right_arithmetic(k, BF16_BITS - 1) & (bias16 - 1)), BF16_BITS)

        def coarse(p, u):
            cand = u | lax.shift_left(jnp.int32(1), BF16_BITS - 1 - p)
            hit = count_rounded(pltpu.bitcast(bf16_bits(cand - bias16), F32)) >= topk
            return jnp.where(hit, cand, u)

        u = lax.fori_loop(0, BF16_BITS, coarse, jnp.zeros((1, tq), I32))
        none = u == 0
        low_bits = bf16_bits(u - bias16 - 1)
        low = low_bits ^ (lax.shift_right_arithmetic(low_bits, 31) & 0x7FFFFFFF)

        def fine(p, d):
            cand = d | lax.shift_left(jnp.int32(1), FINE_BITS - 1 - p)
            return jnp.where(count(as_float(low + cand)) >= topk, cand, d)

        d = lax.fori_loop(0, FINE_BITS, fine, jnp.zeros((1, tq), I32))
        thr = jnp.where(none, jnp.finfo(F32).min, as_float(low + d))
        thr_ref[...] = thr

        n_ge = jnp.where(none, 0, count(thr))

        @pl.when(jnp.max(n_ge) > topk)
        def _():
            need = (topk - count(thr, strict=True)).astype(F32)
            incl = (qcol <= krow).astype(BF16)

            def fix(ci, seen):
                sc = sc_ref[ci]
                tie = sc == thr
                tie_f = jnp.where(tie, 1.0, 0.0)
                run = jnp.dot(incl, tie_f.astype(BF16), preferred_element_type=F32)
                drop = tie & (seen + run > need)
                sc_ref[ci] = jnp.where(drop, -jnp.inf, sc)
                return seen + jnp.sum(tie_f, axis=0, keepdims=True)

            lax.fori_loop(0, n_chunks, fix, jnp.zeros((1, tq), F32))


    def logits_stage(c, par, h, bias_idx):
        hs = slice(h * dh, (h + 1) * dh)
        s = _dot_nt(kbuf_ref[c % KV_SLOTS, :, hs], q_ref[0, :, hs]) + neg_ref[...]
        if bias_idx is not None:
            s += bias_ref[bias_idx, h]
        s_ref[par, h] = s
        cm_ref[par, h] = jnp.max(s, axis=0, keepdims=True)

    def value_stage(c, par, h):
        m_old = m_ref[h]
        m_new = jnp.maximum(m_old, cm_ref[par, h])
        p = jnp.exp2(s_ref[par, h] - m_new).astype(BF16)
        pv = jnp.dot(vbuf_ref[c % KV_SLOTS, h * VROWS:(h + 1) * VROWS, :], p, preferred_element_type=F32)
        acc_ref[h] = jnp.exp2(m_old - m_new) * acc_ref[h] + pv
        m_ref[h] = m_new

    def pipeline_step(c, par, bias_idx, first=False):
        k_copy(c).wait()
        if not first:
            v_copy(c - 1).wait()
            pl.when(c + KV_AHEAD <= last_chunk)(functools.partial(kv_fetch, c + KV_AHEAD))
        neg_ref[...] = jnp.where(sc_ref[c] >= thr_ref[...], 0.0, -jnp.inf)
        for h in range(H):
            logits_stage(c, par, h, bias_idx)
            if not first:
                value_stage(c - 1, 1 - par, h)

    def drain(c, par):
        v_copy(c).wait()
        for h in range(H):
            value_stage(c, par, h)

    score_and_select()

    @pl.when(qi == 0)
    def _():
        pipeline_step(0, 0, 0, first=True)

    @pl.when(qi > 0)
    def _():
        pipeline_step(0, 0, None, first=True)

    def far_pair(i, carry):
        pipeline_step(2 * i + 1, 1, None)
        pipeline_step(2 * i + 2, 0, None)
        return carry

    lax.fori_loop(0, (qi - 1) // 2, far_pair, 0)
    qi_odd = qi % 2 == 1

    @pl.when((qi >= 2) & ~qi_odd)
    def _():
        pipeline_step(qi - 1, 1, None)

    @pl.when(qi_odd)
    def _():
        pipeline_step(qi, 1, 2)
        pipeline_step(qi + 1, 0, 1)
        drain(qi + 1, 0)

    @pl.when((qi >= 2) & ~qi_odd)
    def _():
        pipeline_step(qi, 0, 2)

    @pl.when(~qi_odd)
    def _():
        pipeline_step(qi + 1, 1, 1)
        drain(qi + 1, 1)

    for h in range(H):
        out_t = acc_ref[h, :dh, :] / acc_ref[h, dh:dh + 1, :]
        yb_ref[0, :, h * dh:(h + 1) * dh] = out_t.T.astype(yb_ref.dtype)


def _dsa(z3, vt, zi3, k_meta, ik, rel_bias, *, tq, topk):
    nb, seq, _ = z3.shape
    H, dh = DSA_HEADS, DSA_DH
    nq = seq // tq
    n_chunks = nq + 1
    hd = H * dh
    q_blk = ZG_W // hd
    hbm = pl.BlockSpec(memory_space=pl.ANY)
    return pl.pallas_call(
        functools.partial(_dsa_kernel, tq=tq, topk=topk),
        grid=(nb, nq),
        in_specs=[pl.BlockSpec((1, tq, hd), lambda b, i: (b, i, q_blk)),
                  pl.BlockSpec((1, tq, IQ_W + SMALL_W), lambda b, i: (b, i, 0)),
                  pl.BlockSpec((1, n_chunks, tq, IDX_DH), lambda b, i: (b, 0, 0, 0)),
                  pl.BlockSpec(memory_space=pltpu.SMEM),
                  hbm, hbm, hbm],
        out_specs=pl.BlockSpec((1, tq, hd), lambda b, i: (b, i, 0)),
        out_shape=jax.ShapeDtypeStruct((nb, seq, hd), BF16),
        scratch_shapes=[pltpu.VMEM((n_chunks + 1, tq, tq), F32),
                        pltpu.VMEM((n_chunks + 1, tq, tq), BF16),
                        pltpu.VMEM((1, tq), F32),
                        pltpu.VMEM((tq, tq), F32),
                        pltpu.VMEM((2, H, tq, tq), F32),
                        pltpu.VMEM((2, H, 1, tq), F32),
                        pltpu.VMEM((H, 1, tq), F32),
                        pltpu.VMEM((H, VROWS, tq), F32),
                        pltpu.VMEM((3, H, tq, tq), F32),
                        pltpu.VMEM((KV_SLOTS, tq, hd), BF16),
                        pltpu.VMEM((KV_SLOTS, H * VROWS, tq), BF16),
                        pltpu.SemaphoreType.DMA((2, KV_SLOTS))],
        compiler_params=_params("arbitrary", "arbitrary"),
        name="dsa",
    )(z3, zi3, ik, rel_bias, k_meta, z3, vt)


def _merge_kernel(h_ref, ya_ref, yb_ref, ga_ref, gb_ref, wa_ref, wb_ref, wo_ref, o_ref):
    ua = jnp.dot(ya_ref[...], wa_ref[...], preferred_element_type=F32)
    ub = jnp.dot(yb_ref[...], wb_ref[...], preferred_element_type=F32)
    merged = _sigmoid(ga_ref[...].astype(F32)) * ua + _sigmoid(gb_ref[...].astype(F32)) * ub
    o_ref[...] = h_ref[...] + jnp.dot(merged.astype(BF16), wo_ref[...], preferred_element_type=F32)


def _merge(h, ya, yb, z, wa, wb, wo, *, tm):
    n, d = h.shape
    gate_blk = (ZG_W + ZD_W) // d
    row = pl.BlockSpec((tm, d), lambda i: (i, 0))
    wspec = pl.BlockSpec((d, d), lambda i: (0, 0))
    return pl.pallas_call(
        _merge_kernel,
        grid=(n // tm,),
        in_specs=[row, row, row,
                  pl.BlockSpec((tm, d), lambda i: (i, gate_blk)),
                  pl.BlockSpec((tm, d), lambda i: (i, gate_blk + 1)),
                  wspec, wspec, wspec],
        out_specs=row,
        out_shape=jax.ShapeDtypeStruct((n, d), F32),
        compiler_params=_params("parallel"),
        name="merge",
    )(h, ya, yb, z, z, wa, wb, wo)


def kernel(x, meta_tokens, norm_ffn1, ffn1_w_gate, ffn1_w_up, ffn1_w_down, norm_mix, w_in, gla_w_lr, gla_b_lr,
           gla_norm, w_up_a, w_up_b, w_out, rel_bias, norm_ffn2, ffn2_w_gate, ffn2_w_up, ffn2_w_down, norm_final):
    nb, seq, d = x.shape
    assert norm_ffn1.shape[0] == 1, "single-layer block only"
    assert meta_tokens.shape[0] == N_META
    tq = 256
    assert seq % tq == 0 and (ZG_W + ZD_W) % d == 0 and ZG_W % (DSA_HEADS * DSA_DH) == 0
    topk = min(TOPK_MAX, seq // 4)
    dff = ffn1_w_gate.shape[2]
    tf = dff // 2
    bf = lambda a: a.astype(BF16)

    sizes = (GLA_HEADS * GLA_DK, GLA_HEADS * GLA_DK, GLA_HEADS * GLA_DV, GLA_HEADS * GLA_DV, GLA_LOWRANK,
             DSA_HEADS * DSA_DH, DSA_HEADS * DSA_DH, DSA_HEADS * DSA_DH, IQ_W, IDX_DH, IDX_HEADS, d, d)
    offs = np.concatenate([[0], np.cumsum(sizes)])
    (c_gq, c_gk, c_gv, c_gr, c_lr, c_dq, c_dk, c_dv, c_iq, c_ik, c_iw, c_ga, c_gb) = [
        w_in[0][:, offs[i]:offs[i + 1]] for i in range(len(sizes))]
    c_gq = c_gq * (GLA_DK ** -0.5)
    c_dq = c_dq * (DSA_DH ** -0.5 * LOG2E)
    c_iw = c_iw * ((IDX_HEADS ** -0.5) * (IDX_DH ** -0.5))
    w_big = bf(jnp.concatenate([c_gq, c_gk, c_gv, c_gr, c_dq, c_dk, c_ga, c_gb], axis=1))
    w_dv_t = bf(c_dv.T)
    pad_w = SMALL_W - (IDX_DH + IDX_HEADS + GLA_LOWRANK)
    w_small = bf(jnp.concatenate([c_iq, c_ik, c_iw, c_lr, jnp.zeros((d, pad_w), F32)], axis=1))
    wlr_pad = jnp.zeros((SMALL_W, GLA_HEADS * GLA_DK), F32).at[LR_OFF:LR_OFF + GLA_LOWRANK].set(gla_w_lr[0])

    xr = x.reshape(nb * seq, d)
    g1, gm = norm_ffn1, norm_mix
    wg1, wu1, wd1 = bf(ffn1_w_gate[0]), bf(ffn1_w_up[0]), bf(ffn1_w_down[0])

    h1 = _ffn(xr, g1, wg1, wu1, wd1, tm=1024, tf=tf)
    h1m = _ffn(meta_tokens, g1, wg1, wu1, wd1, tm=512, tf=tf)
    tn = w_big.shape[1] // 4
    z, zi = _proj(h1, gm, w_big, w_small, tm=1024, tn=tn, name="proj")
    zm, zim = _proj(h1m, gm, w_big, w_small, tm=1024, tn=tn, name="proj_meta")

    z3 = z.reshape(nb, seq, z.shape[1])
    zi3 = zi.reshape(nb, seq, zi.shape[1])

    ya = _gla(z3, zi3, zm, zim, wlr_pad, gla_b_lr, gla_norm, chunk=256)

    hd = DSA_HEADS * DSA_DH
    n_chunks = seq // tq + 1
    meta_chunk = lambda a: jnp.pad(a, ((0, tq - N_META), (0, 0)))
    with_meta = lambda m, xs: jnp.concatenate([jnp.broadcast_to(meta_chunk(m)[None], (nb, tq, m.shape[1])), xs], 1)
    k_meta = meta_chunk(zm[:, ZG_W + hd:ZG_W + 2 * hd])
    vt = _values_t(h1, meta_chunk(h1m), gm, w_dv_t, nb=nb, tq=tq)
    ik_all = with_meta(zim[:, IQ_W + IK_OFF:IQ_W + IK_OFF + IDX_DH], zi3[:, :, IQ_W + IK_OFF:IQ_W + IK_OFF + IDX_DH])
    ik = bf(ik_all.reshape(nb, n_chunks, tq, IDX_DH))
    yb = _dsa(z3, vt, zi3, k_meta, ik, rel_bias, tq=tq, topk=topk)

    h2 = _merge(h1, ya.reshape(nb * seq, -1), yb.reshape(nb * seq, -1), z,
                bf(w_up_a[0]), bf(w_up_b[0]), bf(w_out[0]), tm=512)
    out = _ffn(h2, norm_ffn2, bf(ffn2_w_gate[0]), bf(ffn2_w_up[0]), bf(ffn2_w_down[0]), norm_final[None, :],
               tm=1024, tf=tf)
    return out.reshape(nb, seq, d)
```

```python
import functools

import numpy as np
import jax
import jax.numpy as jnp
from jax import lax
from jax.experimental import pallas as pl
from jax.experimental.pallas import tpu as pltpu

F32 = jnp.float32
BF16 = jnp.bfloat16
I32 = jnp.int32
HIGHEST = lax.Precision.HIGHEST

N_META = 16
EPS = 1e-6
GLA_HEADS, GLA_DK, GLA_DV, GLA_LOWRANK, GLA_TAU = 4, 128, 256, 16, 16.0
DSA_HEADS, DSA_DH, IDX_HEADS, IDX_DH, TOPK_MAX = 8, 128, 8, 64, 256
REL_BUCKETS, REL_MAX_DIST = 32, 128

LANES = 128
SUBLANES = 8
VMEM_LIMIT = 56 * 1024 * 1024
INT_MIN = -(2 ** 31)
LOG2E = float(np.log2(np.e))
BF16_SUBLANES = 16
BF16_BITS = 16
FINE_BITS = 18
ONE16 = np.int16(1)
ZERO16 = np.int16(0)
VROWS = DSA_DH + BF16_SUBLANES
KV_SLOTS = 8
KV_AHEAD = 5
assert KV_AHEAD + 2 <= KV_SLOTS
DSA_LOGIT_BUFFERS = 8

ZG_W = 2 * GLA_HEADS * GLA_DK + 2 * GLA_HEADS * GLA_DV
ZD_W = 2 * DSA_HEADS * DSA_DH
SMALL_W = LANES
IQ_W = IDX_HEADS * IDX_DH
IK_OFF, IW_OFF, LR_OFF = 0, IDX_DH, IDX_DH + IDX_HEADS


def _rel_bucket_thresholds():
    n = np.arange(0, 4 * REL_MAX_DIST)
    max_exact = REL_BUCKETS // 2
    nf = np.maximum(n, max_exact).astype(np.float32)
    large = max_exact + (np.log(nf / np.float32(max_exact)) / np.float32(np.log(REL_MAX_DIST / max_exact))
                         * np.float32(REL_BUCKETS - max_exact)).astype(np.int32)
    large64 = max_exact + (np.log(np.maximum(n, max_exact) / max_exact) / np.log(REL_MAX_DIST / max_exact)
                           * (REL_BUCKETS - max_exact)).astype(np.int64)
    assert (large == large64).all()
    bucket = np.where(n < max_exact, n, np.minimum(large, REL_BUCKETS - 1))
    assert (np.diff(bucket) >= 0).all()
    return [int(np.argmax(bucket >= b)) for b in range(REL_BUCKETS)]


BUCKET_START = _rel_bucket_thresholds()


def _rms(x, g):
    return x * lax.rsqrt(jnp.mean(x * x, axis=-1, keepdims=True) + EPS) * g


def _sigmoid(x):
    return 1.0 / (1.0 + jnp.exp(-x))


def _dot_nt(a, b):
    return lax.dot_general(a, b, (((1,), (1,)), ((), ())), preferred_element_type=F32)


def _params(*sem):
    return pltpu.CompilerParams(dimension_semantics=sem, vmem_limit_bytes=VMEM_LIMIT)


def _ffn_kernel(*refs, final_norm):
    if final_norm:
        x_ref, g_ref, wg_ref, wu_ref, wd_ref, gf_ref, o_ref, xn_ref = refs
    else:
        x_ref, g_ref, wg_ref, wu_ref, wd_ref, o_ref, xn_ref = refs
    j = pl.program_id(1)

    @pl.when(j == 0)
    def _():
        xn_ref[...] = _rms(x_ref[...], g_ref[...]).astype(BF16)
        o_ref[...] = jnp.zeros_like(o_ref)

    xn = xn_ref[...]
    a = jnp.dot(xn, wg_ref[...], preferred_element_type=F32)
    b = jnp.dot(xn, wu_ref[...], preferred_element_type=F32)
    hmid = (a * _sigmoid(a) * b).astype(BF16)
    o_ref[...] += jnp.dot(hmid, wd_ref[...], preferred_element_type=F32)

    @pl.when(j == pl.num_programs(1) - 1)
    def _():
        h = x_ref[...] + 0.5 * o_ref[...]
        if final_norm:
            h = _rms(h, gf_ref[...])
        o_ref[...] = h


def _ffn(x, g, wg, wu, wd, gf=None, *, tm, tf):
    n, d = x.shape
    dff = wg.shape[1]
    tm = min(tm, n)
    row = pl.BlockSpec((tm, d), lambda i, j: (i, 0))
    vec = pl.BlockSpec((1, d), lambda i, j: (0, 0))
    in_specs = [row, vec,
                pl.BlockSpec((d, tf), lambda i, j: (0, j)),
                pl.BlockSpec((d, tf), lambda i, j: (0, j)),
                pl.BlockSpec((tf, d), lambda i, j: (j, 0))]
    args = [x, g, wg, wu, wd]
    if gf is not None:
        in_specs.append(vec)
        args.append(gf)
    return pl.pallas_call(
        functools.partial(_ffn_kernel, final_norm=gf is not None),
        grid=(n // tm, dff // tf),
        in_specs=in_specs,
        out_specs=row,
        out_shape=jax.ShapeDtypeStruct((n, d), F32),
        scratch_shapes=[pltpu.VMEM((tm, d), BF16)],
        compiler_params=_params("parallel", "arbitrary"),
        name="ffn_final" if gf is not None else "ffn",
    )(*args)


def _proj_kernel(x_ref, g_ref, w_ref, ws_ref, o_ref, os_ref, xn_ref):
    @pl.when(pl.program_id(1) == 0)
    def _():
        xn = _rms(x_ref[...], g_ref[...]).astype(BF16)
        xn_ref[...] = xn
        os_ref[...] = jnp.dot(xn, ws_ref[...], preferred_element_type=F32)

    o_ref[...] = jnp.dot(xn_ref[...], w_ref[...], preferred_element_type=F32).astype(o_ref.dtype)


def _proj(x, g, w, ws, *, tm, tn, name):
    n, d = x.shape
    width, small = w.shape[1], ws.shape[1]
    tm = min(tm, n)
    return pl.pallas_call(
        _proj_kernel,
        grid=(n // tm, width // tn),
        in_specs=[pl.BlockSpec((tm, d), lambda i, j: (i, 0)),
                  pl.BlockSpec((1, d), lambda i, j: (0, 0)),
                  pl.BlockSpec((d, tn), lambda i, j: (0, j)),
                  pl.BlockSpec((d, small), lambda i, j: (0, 0))],
        out_specs=[pl.BlockSpec((tm, tn), lambda i, j: (i, j)),
                   pl.BlockSpec((tm, small), lambda i, j: (i, 0))],
        out_shape=[jax.ShapeDtypeStruct((n, width), BF16), jax.ShapeDtypeStruct((n, small), F32)],
        scratch_shapes=[pltpu.VMEM((tm, d), BF16)],
        compiler_params=_params("parallel", "arbitrary"),
        name=name,
    )(x, g, w, ws)


def _values_t_kernel(x_ref, xm_ref, g_ref, w_ref, o_ref):
    def emit(x):
        xn = _rms(x, g_ref[...]).astype(BF16)
        vt = _dot_nt(w_ref[...], xn).astype(o_ref.dtype)
        ones = jnp.ones((BF16_SUBLANES, vt.shape[1]), o_ref.dtype)
        for h in range(DSA_HEADS):
            o_ref[0, 0, h * VROWS:h * VROWS + DSA_DH, :] = vt[h * DSA_DH:(h + 1) * DSA_DH]
            o_ref[0, 0, h * VROWS + DSA_DH:(h + 1) * VROWS, :] = ones

    pl.when(pl.program_id(1) == 0)(lambda: emit(xm_ref[...]))
    pl.when(pl.program_id(1) > 0)(lambda: emit(x_ref[...]))


def _values_t(x, x_meta, g, w_t, *, nb, tq):
    n, d = x.shape
    nq = n // nb // tq
    return pl.pallas_call(
        _values_t_kernel,
        grid=(nb, nq + 1),
        in_specs=[pl.BlockSpec((tq, d), lambda b, c: (b * nq + jnp.maximum(c - 1, 0), 0)),
                  pl.BlockSpec((tq, d), lambda b, c: (0, 0)),
                  pl.BlockSpec((1, d), lambda b, c: (0, 0)),
                  pl.BlockSpec(w_t.shape, lambda b, c: (0, 0))],
        out_specs=pl.BlockSpec((1, 1, DSA_HEADS * VROWS, tq), lambda b, c: (b, c, 0, 0)),
        out_shape=jax.ShapeDtypeStruct((nb, nq + 1, DSA_HEADS * VROWS, tq), BF16),
        compiler_params=_params("parallel", "arbitrary"),
        name="values_t",
    )(x, x_meta, g, w_t)


def _lower_tri(n):
    r = lax.broadcasted_iota(I32, (n, n), 0)
    c = lax.broadcasted_iota(I32, (n, n), 1)
    return c <= r


def _gla_kernel(zg_ref, zs_ref, zgm_ref, zsm_ref, wlr_ref, blr_ref, gn_ref, ya_ref, st_ref, *, nb, chunk):
    H, dk, dv = GLA_HEADS, GLA_DK, GLA_DV
    kw = H * dk
    wlr = wlr_ref[...]
    blr = blr_ref[...]
    gn = gn_ref[...]

    def log_decay(zs):
        x = jnp.dot(zs, wlr, preferred_element_type=F32, precision=HIGHEST) + blr
        return (jnp.minimum(x, 0.0) - jnp.log(1.0 + jnp.exp(-jnp.abs(x)))) * (1.0 / GLA_TAU)

    def cumsum_rows(g):
        tri = _lower_tri(g.shape[0]).astype(F32)
        return jnp.dot(tri, g, preferred_element_type=F32, precision=HIGHEST)

    @pl.when(pl.program_id(0) == 0)
    def _():
        bc = cumsum_rows(log_decay(zsm_ref[...]))
        bl = bc[N_META - 1:N_META, :]
        zgm = zgm_ref[...]
        kl = (zgm[:, kw:2 * kw].astype(F32) * jnp.exp(bl - bc)).astype(BF16)
        v = zgm[:, 2 * kw:2 * kw + H * dv]
        for h in range(H):
            st = lax.dot_general(v[:, h * dv:(h + 1) * dv], kl[:, h * dk:(h + 1) * dk],
                                 (((0,), (0,)), ((), ())), preferred_element_type=F32)
            for b in range(nb):
                st_ref[b * H + h] = st

    causal = _lower_tri(chunk)
    mid = chunk // 2
    for b in range(nb):
        zg = zg_ref[b]
        bc = cumsum_rows(log_decay(zs_ref[b]))
        bl = bc[chunk - 1:chunk, :]
        bm = bc[mid - 1:mid, :]
        qe = zg[:, :kw].astype(F32) * jnp.exp(bc - bm)
        ke = zg[:, kw:2 * kw].astype(F32) * jnp.exp(bm - bc)
        qs = (qe * jnp.exp(bm)).astype(BF16)
        kl = (ke * jnp.exp(bl - bm)).astype(BF16)
        qe = qe.astype(BF16)
        ke = ke.astype(BF16)
        dec = jnp.exp(bl)
        for h in range(H):
            ks = slice(h * dk, (h + 1) * dk)
            v = zg[:, 2 * kw + h * dv:2 * kw + (h + 1) * dv]
            r = zg[:, 2 * kw + H * dv + h * dv:2 * kw + H * dv + (h + 1) * dv].astype(F32)
            a = jnp.where(causal, _dot_nt(qe[:, ks], ke[:, ks]), 0.0).astype(BF16)
            st = st_ref[b * H + h]
            o = jnp.dot(a, v, preferred_element_type=F32) + _dot_nt(qs[:, ks], st.astype(BF16))
            st_ref[b * H + h] = st * dec[:, ks] + lax.dot_general(
                v, kl[:, ks], (((0,), (0,)), ((), ())), preferred_element_type=F32)
            y = _rms(o, gn) * (r * _sigmoid(r))
            ya_ref[b, :, h * dv:(h + 1) * dv] = y.astype(ya_ref.dtype)


def _gla(z3, zi3, z_meta, zi_meta, wlr_pad, blr, gn, *, chunk):
    nb, seq, _ = z3.shape
    H, dk, dv = GLA_HEADS, GLA_DK, GLA_DV
    small_blk = IQ_W // SMALL_W
    return pl.pallas_call(
        functools.partial(_gla_kernel, nb=nb, chunk=chunk),
        grid=(seq // chunk,),
        in_specs=[pl.BlockSpec((nb, chunk, ZG_W), lambda c: (0, c, 0)),
                  pl.BlockSpec((nb, chunk, SMALL_W), lambda c: (0, c, small_blk)),
                  pl.BlockSpec((N_META, ZG_W), lambda c: (0, 0)),
                  pl.BlockSpec((N_META, SMALL_W), lambda c: (0, small_blk)),
                  pl.BlockSpec((SMALL_W, H * dk), lambda c: (0, 0)),
                  pl.BlockSpec((1, H * dk), lambda c: (0, 0)),
                  pl.BlockSpec((1, dv), lambda c: (0, 0))],
        out_specs=pl.BlockSpec((nb, chunk, H * dv), lambda c: (0, c, 0)),
        out_shape=jax.ShapeDtypeStruct((nb, seq, H * dv), BF16),
        scratch_shapes=[pltpu.VMEM((nb * H, dv, dk), F32)],
        compiler_params=_params("arbitrary"),
        name="gla",
    )(z3, zi3, z_meta, zi_meta, wlr_pad, blr, gn)


def _dsa_kernel(q_ref, zi_ref, ik_ref, relb_ref, km_hbm, z_hbm, vt_hbm, yb_ref,
                sc_ref, sb_ref, thr_ref, neg_ref, s_ref, cm_ref, m_ref, acc_ref, bias_ref,
                kbuf_ref, vbuf_ref, sem_ref, *, tq, topk):
    H, dh = DSA_HEADS, DSA_DH
    hd = H * dh
    b_id, qi = pl.program_id(0), pl.program_id(1)
    n_chunks = qi + 2
    krow = lax.broadcasted_iota(I32, (tq, tq), 0)
    qcol = lax.broadcasted_iota(I32, (tq, tq), 1)

    last_chunk = qi + 1

    def k_copy(c):
        slot = c % KV_SLOTS
        if isinstance(c, int) and c == 0:
            src = km_hbm
        else:
            start = (c - 1) * tq
            rows = pl.ds(start if isinstance(start, int) else pl.multiple_of(start, tq), tq)
            src = z_hbm.at[b_id, rows, pl.ds(ZG_W + hd, hd)]
        return pltpu.make_async_copy(src, kbuf_ref.at[slot], sem_ref.at[0, slot])

    def v_copy(c):
        slot = c % KV_SLOTS
        return pltpu.make_async_copy(vt_hbm.at[b_id, c], vbuf_ref.at[slot], sem_ref.at[1, slot])

    def kv_fetch(c):
        k_copy(c).start()
        v_copy(c).start()

    kv_fetch(0)
    kv_fetch(1)
    for c in range(2, min(KV_AHEAD, ik_ref.shape[1] - 1) + 1):
        pl.when(c <= last_chunk)(functools.partial(kv_fetch, c))

    @pl.when((b_id == 0) & (qi == 0))
    def _():
        for t, dist in enumerate((N_META + qcol - krow, qcol - krow, tq + qcol - krow)):
            for h in range(H):
                far = relb_ref[REL_BUCKETS - 1, h]
                tile = jnp.full((tq, tq), (relb_ref[0, h] - far) * LOG2E, F32)
                for bkt in range(1, REL_BUCKETS):
                    tile = jnp.where(dist >= BUCKET_START[bkt], (relb_ref[bkt, h] - far) * LOG2E, tile)
                bias_ref[t, h] = tile

    def score_and_select():
        zi = zi_ref[0]
        m_ref[...] = jnp.full(m_ref.shape, -1e30, F32)
        acc_ref[...] = jnp.zeros(acc_ref.shape, F32)

        iq = [zi[:, h * IDX_DH:(h + 1) * IDX_DH].astype(BF16) for h in range(IDX_HEADS)]
        small_t = zi[:, IQ_W:].T
        iw = [small_t[IW_OFF + h:IW_OFF + h + 1, :] for h in range(IDX_HEADS)]

        def scores(ci):
            ikc = ik_ref[0, ci]
            sc = jnp.zeros((tq, tq), F32)
            for h in range(IDX_HEADS):
                sc += iw[h] * jnp.maximum(_dot_nt(ikc, iq[h]), 0.0)
            return sc

        def store_scores(ci, sc):
            sc_ref[ci] = sc
            sb_ref[ci] = sc.astype(BF16)

        store_scores(0, jnp.where(krow < N_META, scores(0), -jnp.inf))

        def full_chunk(ci, carry):
            store_scores(ci, scores(ci))
            return carry

        lax.fori_loop(1, qi + 1, full_chunk, 0)
        store_scores(qi + 1, jnp.where(krow <= qcol, scores(qi + 1), -jnp.inf))
        store_scores(n_chunks, jnp.full((tq, tq), -jnp.inf, F32))
        n_pairs = (n_chunks + 1) // 2

        def as_float(key):
            return pltpu.bitcast(key ^ (lax.shift_right_arithmetic(key, 31) & 0x7FFFFFFF), F32)

        def count_rounded(cand):
            cand16 = jnp.broadcast_to(cand, (BF16_SUBLANES, tq)).astype(BF16)

            def body(i, accs):
                accs = list(accs)
                for ci in (2 * i, 2 * i + 1):
                    for j in range(tq // BF16_SUBLANES):
                        blk = sb_ref[ci, j * BF16_SUBLANES:(j + 1) * BF16_SUBLANES, :]
                        accs[j % len(accs)] = accs[j % len(accs)] + jnp.where(blk >= cand16, ONE16, ZERO16)
                return tuple(accs)

            accs = lax.fori_loop(0, n_pairs, body,
                                 tuple(jnp.zeros((BF16_SUBLANES, tq), jnp.int16) for _ in range(4)))
            tot = sum((a.astype(I32) for a in accs[1:]), accs[0].astype(I32))
            return jnp.sum(tot, axis=0, keepdims=True)

        def count(cand, strict=False):
            cand8 = jnp.broadcast_to(cand, (SUBLANES, tq))

            def body(i, accs):
                accs = list(accs)
                for ci in (2 * i, 2 * i + 1):
                    for j in range(tq // SUBLANES):
                        blk = sc_ref[ci, j * SUBLANES:(j + 1) * SUBLANES, :]
                        hit = blk > cand8 if strict else blk >= cand8
                        accs[j % len(accs)] = accs[j % len(accs)] + jnp.where(hit, 1, 0)
                return tuple(accs)

            accs = lax.fori_loop(0, n_pairs, body, tuple(jnp.zeros((SUBLANES, tq), I32) for _ in range(4)))
            return jnp.sum(sum(accs[1:], accs[0]), axis=0, keepdims=True)

        bias16 = 1 << (BF16_BITS - 1)

        def bf16_bits(k):
            return lax.shift_left(k ^ (lax.shift_right_arithmetic(k, BF16_BITS - 1) & (bias16 - 1)), BF16_BITS)

        def coarse(p, u):
            cand = u | lax.shift_left(jnp.int32(1), BF16_BITS - 1 - p)
            hit = count_rounded(pltpu.bitcast(bf16_bits(cand - bias16), F32)) >= topk
            return jnp.where(hit, cand, u)

        u = lax.fori_loop(0, BF16_BITS, coarse, jnp.zeros((1, tq), I32))
        none = u == 0
        low_bits = bf16_bits(u - bias16 - 1)
        low = low_bits ^ (lax.shift_right_arithmetic(low_bits, 31) & 0x7FFFFFFF)

        def fine(p, d):
            cand = d | lax.shift_left(jnp.int32(1), FINE_BITS - 1 - p)
            return jnp.where(count(as_float(low + cand)) >= topk, cand, d)

        d = lax.fori_loop(0, FINE_BITS, fine, jnp.zeros((1, tq), I32))
        thr = jnp.where(none, jnp.finfo(F32).min, as_float(low + d))
        thr_ref[...] = thr

        n_ge = jnp.where(none, 0, count(thr))

        @pl.when(jnp.max(n_ge) > topk)
        def _():
            need = (topk - count(thr, strict=True)).astype(F32)
            incl = (qcol <= krow).astype(BF16)

            def fix(ci, seen):
                sc = sc_ref[ci]
                tie = sc == thr
                tie_f = jnp.where(tie, 1.0, 0.0)
                run = jnp.dot(incl, tie_f.astype(BF16), preferred_element_type=F32)
                drop = tie & (seen + run > need)
                sc_ref[ci] = jnp.where(drop, -jnp.inf, sc)
                return seen + jnp.sum(tie_f, axis=0, keepdims=True)

            lax.fori_loop(0, n_chunks, fix, jnp.zeros((1, tq), F32))


    def logits_stage(c, par, h, bias_idx):
        hs = slice(h * dh, (h + 1) * dh)
        s = _dot_nt(kbuf_ref[c % KV_SLOTS, :, hs], q_ref[0, :, hs]) + neg_ref[...]
        if bias_idx is not None:
            s += bias_ref[bias_idx, h]
        s_ref[par, h] = s
        cm_ref[par, h] = jnp.max(s, axis=0, keepdims=True)

    def value_stage(c, par, h):
        m_old = m_ref[h]
        m_new = jnp.maximum(m_old, cm_ref[par, h])
        p = jnp.exp2(s_ref[par, h] - m_new).astype(BF16)
        pv = jnp.dot(vbuf_ref[c % KV_SLOTS, h * VROWS:(h + 1) * VROWS, :], p, preferred_element_type=F32)
        acc_ref[h] = jnp.exp2(m_old - m_new) * acc_ref[h] + pv
        m_ref[h] = m_new

    def pipeline_step(c, par, bias_idx, first=False):
        k_copy(c).wait()
        if not first:
            v_copy(c - 1).wait()
            pl.when(c + KV_AHEAD <= last_chunk)(functools.partial(kv_fetch, c + KV_AHEAD))
        neg_ref[...] = jnp.where(sc_ref[c] >= thr_ref[...], 0.0, -jnp.inf)
        for h in range(H):
            logits_stage(c, par, h, bias_idx)
            if not first:
                value_stage(c - 1, 1 - par, h)

    def drain(c, par):
        v_copy(c).wait()
        for h in range(H):
            value_stage(c, par, h)

    score_and_select()

    @pl.when(qi == 0)
    def _():
        pipeline_step(0, 0, 0, first=True)

    @pl.when(qi > 0)
    def _():
        pipeline_step(0, 0, None, first=True)

    def far_pair(i, carry):
        pipeline_step(2 * i + 1, 1, None)
        pipeline_step(2 * i + 2, 0, None)
        return carry

    lax.fori_loop(0, (qi - 1) // 2, far_pair, 0)
    qi_odd = qi % 2 == 1

    @pl.when((qi >= 2) & ~qi_odd)
    def _():
        pipeline_step(qi - 1, 1, None)

    @pl.when(qi_odd)
    def _():
        pipeline_step(qi, 1, 2)
        pipeline_step(qi + 1, 0, 1)
        drain(qi + 1, 0)

    @pl.when((qi >= 2) & ~qi_odd)
    def _():
        pipeline_step(qi, 0, 2)

    @pl.when(~qi_odd)
    def _():
        pipeline_step(qi + 1, 1, 1)
        drain(qi + 1, 1)

    for h in range(H):
        out_t = acc_ref[h, :dh, :] / acc_ref[h, dh:dh + 1, :]
        yb_ref[0, :, h * dh:(h + 1) * dh] = out_t.T.astype(yb_ref.dtype)


def _dsa(z3, vt, zi3, k_meta, ik, rel_bias, *, tq, topk):
    nb, seq, _ = z3.shape
    H, dh = DSA_HEADS, DSA_DH
    nq = seq // tq
    n_chunks = nq + 1
    hd = H * dh
    q_blk = ZG_W // hd
    hbm = pl.BlockSpec(memory_space=pl.ANY)
    return pl.pallas_call(
        functools.partial(_dsa_kernel, tq=tq, topk=topk),
        grid=(nb, nq),
        in_specs=[pl.BlockSpec((1, tq, hd), lambda b, i: (b, i, q_blk)),
                  pl.BlockSpec((1, tq, IQ_W + SMALL_W), lambda b, i: (b, i, 0)),
                  pl.BlockSpec((1, n_chunks, tq, IDX_DH), lambda b, i: (b, 0, 0, 0)),
                  pl.BlockSpec(memory_space=pltpu.SMEM),
                  hbm, hbm, hbm],
        out_specs=pl.BlockSpec((1, tq, hd), lambda b, i: (b, i, 0)),
        out_shape=jax.ShapeDtypeStruct((nb, seq, hd), BF16),
        scratch_shapes=[pltpu.VMEM((n_chunks + 1, tq, tq), F32),
                        pltpu.VMEM((n_chunks + 1, tq, tq), BF16),
                        pltpu.VMEM((1, tq), F32),
                        pltpu.VMEM((tq, tq), F32),
                        pltpu.VMEM((2, H, tq, tq), F32),
                        pltpu.VMEM((2, H, 1, tq), F32),
                        pltpu.VMEM((H, 1, tq), F32),
                        pltpu.VMEM((H, VROWS, tq), F32),
                        pltpu.VMEM((3, H, tq, tq), F32),
                        pltpu.VMEM((KV_SLOTS, tq, hd), BF16),
                        pltpu.VMEM((KV_SLOTS, H * VROWS, tq), BF16),
                        pltpu.SemaphoreType.DMA((2, KV_SLOTS))],
        compiler_params=_params("arbitrary", "arbitrary"),
        name="dsa",
    )(z3, zi3, ik, rel_bias, k_meta, z3, vt)


def _merge_kernel(h_ref, ya_ref, yb_ref, ga_ref, gb_ref, wa_ref, wb_ref, wo_ref, o_ref):
    ua = jnp.dot(ya_ref[...], wa_ref[...], preferred_element_type=F32)
    ub = jnp.dot(yb_ref[...], wb_ref[...], preferred_element_type=F32)
    merged = _sigmoid(ga_ref[...].astype(F32)) * ua + _sigmoid(gb_ref[...].astype(F32)) * ub
    o_ref[...] = h_ref[...] + jnp.dot(merged.astype(BF16), wo_ref[...], preferred_element_type=F32)


def _merge(h, ya, yb, z, wa, wb, wo, *, tm):
    n, d = h.shape
    gate_blk = (ZG_W + ZD_W) // d
    row = pl.BlockSpec((tm, d), lambda i: (i, 0))
    wspec = pl.BlockSpec((d, d), lambda i: (0, 0))
    return pl.pallas_call(
        _merge_kernel,
        grid=(n // tm,),
        in_specs=[row, row, row,
                  pl.BlockSpec((tm, d), lambda i: (i, gate_blk)),
                  pl.BlockSpec((tm, d), lambda i: (i, gate_blk + 1)),
                  wspec, wspec, wspec],
        out_specs=row,
        out_shape=jax.ShapeDtypeStruct((n, d), F32),
        compiler_params=_params("parallel"),
        name="merge",
    )(h, ya, yb, z, z, wa, wb, wo)


def kernel(x, meta_tokens, norm_ffn1, ffn1_w_gate, ffn1_w_up, ffn1_w_down, norm_mix, w_in, gla_w_lr, gla_b_lr,
           gla_norm, w_up_a, w_up_b, w_out, rel_bias, norm_ffn2, ffn2_w_gate, ffn2_w_up, ffn2_w_down, norm_final):
    nb, seq, d = x.shape
    assert norm_ffn1.shape[0] == 1, "single-layer block only"
    assert meta_tokens.shape[0] == N_META
    tq = 256
    assert seq % tq == 0 and (ZG_W + ZD_W) % d == 0 and ZG_W % (DSA_HEADS * DSA_DH) == 0
    topk = min(TOPK_MAX, seq // 4)
    dff = ffn1_w_gate.shape[2]
    tf = dff // 2
    bf = lambda a: a.astype(BF16)

    sizes = (GLA_HEADS * GLA_DK, GLA_HEADS * GLA_DK, GLA_HEADS * GLA_DV, GLA_HEADS * GLA_DV, GLA_LOWRANK,
             DSA_HEADS * DSA_DH, DSA_HEADS * DSA_DH, DSA_HEADS * DSA_DH, IQ_W, IDX_DH, IDX_HEADS, d, d)
    offs = np.concatenate([[0], np.cumsum(sizes)])
    (c_gq, c_gk, c_gv, c_gr, c_lr, c_dq, c_dk, c_dv, c_iq, c_ik, c_iw, c_ga, c_gb) = [
        w_in[0][:, offs[i]:offs[i + 1]] for i in range(len(sizes))]
    c_gq = c_gq * (GLA_DK ** -0.5)
    c_dq = c_dq * (DSA_DH ** -0.5 * LOG2E)
    c_iw = c_iw * ((IDX_HEADS ** -0.5) * (IDX_DH ** -0.5))
    w_big = bf(jnp.concatenate([c_gq, c_gk, c_gv, c_gr, c_dq, c_dk, c_ga, c_gb], axis=1))
    w_dv_t = bf(c_dv.T)
    pad_w = SMALL_W - (IDX_DH + IDX_HEADS + GLA_LOWRANK)
    w_small = bf(jnp.concatenate([c_iq, c_ik, c_iw, c_lr, jnp.zeros((d, pad_w), F32)], axis=1))
    wlr_pad = jnp.zeros((SMALL_W, GLA_HEADS * GLA_DK), F32).at[LR_OFF:LR_OFF + GLA_LOWRANK].set(gla_w_lr[0])

    xr = x.reshape(nb * seq, d)
    g1, gm = norm_ffn1, norm_mix
    wg1, wu1, wd1 = bf(ffn1_w_gate[0]), bf(ffn1_w_up[0]), bf(ffn1_w_down[0])

    h1 = _ffn(xr, g1, wg1, wu1, wd1, tm=1024, tf=tf)
    h1m = _ffn(meta_tokens, g1, wg1, wu1, wd1, tm=512, tf=tf)
    tn = w_big.shape[1] // 2
    z, zi = _proj(h1, gm, w_big, w_small, tm=1024, tn=tn, name="proj")
    zm, zim = _proj(h1m, gm, w_big, w_small, tm=1024, tn=tn, name="proj_meta")

    z3 = z.reshape(nb, seq, z.shape[1])
    zi3 = zi.reshape(nb, seq, zi.shape[1])

    ya = _gla(z3, zi3, zm, zim, wlr_pad, gla_b_lr, gla_norm, chunk=256)

    hd = DSA_HEADS * DSA_DH
    n_chunks = seq // tq + 1
    meta_chunk = lambda a: jnp.pad(a, ((0, tq - N_META), (0, 0)))
    with_meta = lambda m, xs: jnp.concatenate([jnp.broadcast_to(meta_chunk(m)[None], (nb, tq, m.shape[1])), xs], 1)
    k_meta = meta_chunk(zm[:, ZG_W + hd:ZG_W + 2 * hd])
    vt = _values_t(h1, meta_chunk(h1m), gm, w_dv_t, nb=nb, tq=tq)
    ik_all = with_meta(zim[:, IQ_W + IK_OFF:IQ_W + IK_OFF + IDX_DH], zi3[:, :, IQ_W + IK_OFF:IQ_W + IK_OFF + IDX_DH])
    ik = bf(ik_all.reshape(nb, n_chunks, tq, IDX_DH))
    yb = _dsa(z3, vt, zi3, k_meta, ik, rel_bias, tq=tq, topk=topk)

    h2 = _merge(h1, ya.reshape(nb * seq, -1), yb.reshape(nb * seq, -1), z,
                bf(w_up_a[0]), bf(w_up_b[0]), bf(w_out[0]), tm=1024)
    out = _ffn(h2, norm_ffn2, bf(ffn2_w_gate[0]), bf(ffn2_w_up[0]), bf(ffn2_w_down[0]), norm_final[None, :],
               tm=1024, tf=tf)
    return out.reshape(nb, seq, d)
```

```python
import functools

import numpy as np
import jax
import jax.numpy as jnp
from jax import lax
from jax.experimental import pallas as pl
from jax.experimental.pallas import tpu as pltpu

F32 = jnp.float32
BF16 = jnp.bfloat16
I32 = jnp.int32
HIGHEST = lax.Precision.HIGHEST

N_META = 16
EPS = 1e-6
GLA_HEADS, GLA_DK, GLA_DV, GLA_LOWRANK, GLA_TAU = 4, 128, 256, 16, 16.0
DSA_HEADS, DSA_DH, IDX_HEADS, IDX_DH, TOPK_MAX = 8, 128, 8, 64, 256
REL_BUCKETS, REL_MAX_DIST = 32, 128

LANES = 128
SUBLANES = 8
VMEM_LIMIT = 56 * 1024 * 1024
INT_MIN = -(2 ** 31)
LOG2E = float(np.log2(np.e))
BF16_SUBLANES = 16
BF16_BITS = 16
FINE_BITS = 18
ONE16 = np.int16(1)
ZERO16 = np.int16(0)
VROWS = DSA_DH + BF16_SUBLANES
KV_SLOTS = 8
KV_AHEAD = 5
assert KV_AHEAD + 2 <= KV_SLOTS
DSA_LOGIT_BUFFERS = 8

ZG_W = 2 * GLA_HEADS * GLA_DK + 2 * GLA_HEADS * GLA_DV
ZD_W = 2 * DSA_HEADS * DSA_DH
SMALL_W = LANES
IQ_W = IDX_HEADS * IDX_DH
IK_OFF, IW_OFF, LR_OFF = 0, IDX_DH, IDX_DH + IDX_HEADS


def _rel_bucket_thresholds():
    n = np.arange(0, 4 * REL_MAX_DIST)
    max_exact = REL_BUCKETS // 2
    nf = np.maximum(n, max_exact).astype(np.float32)
    large = max_exact + (np.log(nf / np.float32(max_exact)) / np.float32(np.log(REL_MAX_DIST / max_exact))
                         * np.float32(REL_BUCKETS - max_exact)).astype(np.int32)
    large64 = max_exact + (np.log(np.maximum(n, max_exact) / max_exact) / np.log(REL_MAX_DIST / max_exact)
                           * (REL_BUCKETS - max_exact)).astype(np.int64)
    assert (large == large64).all()
    bucket = np.where(n < max_exact, n, np.minimum(large, REL_BUCKETS - 1))
    assert (np.diff(bucket) >= 0).all()
    return [int(np.argmax(bucket >= b)) for b in range(REL_BUCKETS)]


BUCKET_START = _rel_bucket_thresholds()


def _rms(x, g):
    return x * lax.rsqrt(jnp.mean(x * x, axis=-1, keepdims=True) + EPS) * g


def _sigmoid(x):
    return 1.0 / (1.0 + jnp.exp(-x))


def _dot_nt(a, b):
    return lax.dot_general(a, b, (((1,), (1,)), ((), ())), preferred_element_type=F32)


def _params(*sem):
    return pltpu.CompilerParams(dimension_semantics=sem, vmem_limit_bytes=VMEM_LIMIT)


def _ffn_kernel(*refs, final_norm):
    if final_norm:
        x_ref, g_ref, wg_ref, wu_ref, wd_ref, gf_ref, o_ref, xn_ref = refs
    else:
        x_ref, g_ref, wg_ref, wu_ref, wd_ref, o_ref, xn_ref = refs
    j = pl.program_id(1)

    @pl.when(j == 0)
    def _():
        xn_ref[...] = _rms(x_ref[...], g_ref[...]).astype(BF16)
        o_ref[...] = jnp.zeros_like(o_ref)

    xn = xn_ref[...]
    a = jnp.dot(xn, wg_ref[...], preferred_element_type=F32)
    b = jnp.dot(xn, wu_ref[...], preferred_element_type=F32)
    hmid = (a * _sigmoid(a) * b).astype(BF16)
    o_ref[...] += jnp.dot(hmid, wd_ref[...], preferred_element_type=F32)

    @pl.when(j == pl.num_programs(1) - 1)
    def _():
        h = x_ref[...] + 0.5 * o_ref[...]
        if final_norm:
            h = _rms(h, gf_ref[...])
        o_ref[...] = h


def _ffn(x, g, wg, wu, wd, gf=None, *, tm, tf):
    n, d = x.shape
    dff = wg.shape[1]
    tm = min(tm, n)
    row = pl.BlockSpec((tm, d), lambda i, j: (i, 0))
    vec = pl.BlockSpec((1, d), lambda i, j: (0, 0))
    in_specs = [row, vec,
                pl.BlockSpec((d, tf), lambda i, j: (0, j)),
                pl.BlockSpec((d, tf), lambda i, j: (0, j)),
                pl.BlockSpec((tf, d), lambda i, j: (j, 0))]
    args = [x, g, wg, wu, wd]
    if gf is not None:
        in_specs.append(vec)
        args.append(gf)
    return pl.pallas_call(
        functools.partial(_ffn_kernel, final_norm=gf is not None),
        grid=(n // tm, dff // tf),
        in_specs=in_specs,
        out_specs=row,
        out_shape=jax.ShapeDtypeStruct((n, d), F32),
        scratch_shapes=[pltpu.VMEM((tm, d), BF16)],
        compiler_params=_params("parallel", "arbitrary"),
        name="ffn_final" if gf is not None else "ffn",
    )(*args)


def _proj_kernel(x_ref, g_ref, w_ref, ws_ref, o_ref, os_ref, xn_ref):
    @pl.when(pl.program_id(1) == 0)
    def _():
        xn = _rms(x_ref[...], g_ref[...]).astype(BF16)
        xn_ref[...] = xn
        os_ref[...] = jnp.dot(xn, ws_ref[...], preferred_element_type=F32)

    o_ref[...] = jnp.dot(xn_ref[...], w_ref[...], preferred_element_type=F32).astype(o_ref.dtype)


def _proj(x, g, w, ws, *, tm, tn, name):
    n, d = x.shape
    width, small = w.shape[1], ws.shape[1]
    tm = min(tm, n)
    return pl.pallas_call(
        _proj_kernel,
        grid=(n // tm, width // tn),
        in_specs=[pl.BlockSpec((tm, d), lambda i, j: (i, 0)),
                  pl.BlockSpec((1, d), lambda i, j: (0, 0)),
                  pl.BlockSpec((d, tn), lambda i, j: (0, j)),
                  pl.BlockSpec((d, small), lambda i, j: (0, 0))],
        out_specs=[pl.BlockSpec((tm, tn), lambda i, j: (i, j)),
                   pl.BlockSpec((tm, small), lambda i, j: (i, 0))],
        out_shape=[jax.ShapeDtypeStruct((n, width), BF16), jax.ShapeDtypeStruct((n, small), F32)],
        scratch_shapes=[pltpu.VMEM((tm, d), BF16)],
        compiler_params=_params("parallel", "arbitrary"),
        name=name,
    )(x, g, w, ws)


def _values_t_kernel(x_ref, xm_ref, g_ref, w_ref, o_ref):
    def emit(x):
        xn = _rms(x, g_ref[...]).astype(BF16)
        vt = _dot_nt(w_ref[...], xn).astype(o_ref.dtype)
        ones = jnp.ones((BF16_SUBLANES, vt.shape[1]), o_ref.dtype)
        for h in range(DSA_HEADS):
            o_ref[0, 0, h * VROWS:h * VROWS + DSA_DH, :] = vt[h * DSA_DH:(h + 1) * DSA_DH]
            o_ref[0, 0, h * VROWS + DSA_DH:(h + 1) * VROWS, :] = ones

    pl.when(pl.program_id(1) == 0)(lambda: emit(xm_ref[...]))
    pl.when(pl.program_id(1) > 0)(lambda: emit(x_ref[...]))


def _values_t(x, x_meta, g, w_t, *, nb, tq):
    n, d = x.shape
    nq = n // nb // tq
    return pl.pallas_call(
        _values_t_kernel,
        grid=(nb, nq + 1),
        in_specs=[pl.BlockSpec((tq, d), lambda b, c: (b * nq + jnp.maximum(c - 1, 0), 0)),
                  pl.BlockSpec((tq, d), lambda b, c: (0, 0)),
                  pl.BlockSpec((1, d), lambda b, c: (0, 0)),
                  pl.BlockSpec(w_t.shape, lambda b, c: (0, 0))],
        out_specs=pl.BlockSpec((1, 1, DSA_HEADS * VROWS, tq), lambda b, c: (b, c, 0, 0)),
        out_shape=jax.ShapeDtypeStruct((nb, nq + 1, DSA_HEADS * VROWS, tq), BF16),
        compiler_params=_params("parallel", "arbitrary"),
        name="values_t",
    )(x, x_meta, g, w_t)


def _lower_tri(n):
    r = lax.broadcasted_iota(I32, (n, n), 0)
    c = lax.broadcasted_iota(I32, (n, n), 1)
    return c <= r


def _gla_kernel(zg_ref, zs_ref, zgm_ref, zsm_ref, wlr_ref, blr_ref, gn_ref, ya_ref, st_ref, *, nb, chunk):
    H, dk, dv = GLA_HEADS, GLA_DK, GLA_DV
    kw = H * dk
    wlr = wlr_ref[...]
    blr = blr_ref[...]
    gn = gn_ref[...]

    def log_decay(zs):
        x = jnp.dot(zs, wlr, preferred_element_type=F32, precision=HIGHEST) + blr
        return (jnp.minimum(x, 0.0) - jnp.log(1.0 + jnp.exp(-jnp.abs(x)))) * (1.0 / GLA_TAU)

    def cumsum_rows(g):
        tri = _lower_tri(g.shape[0]).astype(F32)
        return jnp.dot(tri, g, preferred_element_type=F32, precision=HIGHEST)

    @pl.when(pl.program_id(0) == 0)
    def _():
        bc = cumsum_rows(log_decay(zsm_ref[...]))
        bl = bc[N_META - 1:N_META, :]
        zgm = zgm_ref[...]
        kl = (zgm[:, kw:2 * kw].astype(F32) * jnp.exp(bl - bc)).astype(BF16)
        v = zgm[:, 2 * kw:2 * kw + H * dv]
        for h in range(H):
            st = lax.dot_general(v[:, h * dv:(h + 1) * dv], kl[:, h * dk:(h + 1) * dk],
                                 (((0,), (0,)), ((), ())), preferred_element_type=F32)
            for b in range(nb):
                st_ref[b * H + h] = st

    causal = _lower_tri(chunk)
    mid = chunk // 2
    for b in range(nb):
        zg = zg_ref[b]
        bc = cumsum_rows(log_decay(zs_ref[b]))
        bl = bc[chunk - 1:chunk, :]
        bm = bc[mid - 1:mid, :]
        qe = zg[:, :kw].astype(F32) * jnp.exp(bc - bm)
        ke = zg[:, kw:2 * kw].astype(F32) * jnp.exp(bm - bc)
        qs = (qe * jnp.exp(bm)).astype(BF16)
        kl = (ke * jnp.exp(bl - bm)).astype(BF16)
        qe = qe.astype(BF16)
        ke = ke.astype(BF16)
        dec = jnp.exp(bl)
        for h in range(H):
            ks = slice(h * dk, (h + 1) * dk)
            v = zg[:, 2 * kw + h * dv:2 * kw + (h + 1) * dv]
            r = zg[:, 2 * kw + H * dv + h * dv:2 * kw + H * dv + (h + 1) * dv].astype(F32)
            a = jnp.where(causal, _dot_nt(qe[:, ks], ke[:, ks]), 0.0).astype(BF16)
            st = st_ref[b * H + h]
            o = jnp.dot(a, v, preferred_element_type=F32) + _dot_nt(qs[:, ks], st.astype(BF16))
            st_ref[b * H + h] = st * dec[:, ks] + lax.dot_general(
                v, kl[:, ks], (((0,), (0,)), ((), ())), preferred_element_type=F32)
            y = _rms(o, gn) * (r * _sigmoid(r))
            ya_ref[b, :, h * dv:(h + 1) * dv] = y.astype(ya_ref.dtype)


def _gla(z3, zi3, z_meta, zi_meta, wlr_pad, blr, gn, *, chunk):
    nb, seq, _ = z3.shape
    H, dk, dv = GLA_HEADS, GLA_DK, GLA_DV
    small_blk = IQ_W // SMALL_W
    return pl.pallas_call(
        functools.partial(_gla_kernel, nb=nb, chunk=chunk),
        grid=(seq // chunk,),
        in_specs=[pl.BlockSpec((nb, chunk, ZG_W), lambda c: (0, c, 0)),
                  pl.BlockSpec((nb, chunk, SMALL_W), lambda c: (0, c, small_blk)),
                  pl.BlockSpec((N_META, ZG_W), lambda c: (0, 0)),
                  pl.BlockSpec((N_META, SMALL_W), lambda c: (0, small_blk)),
                  pl.BlockSpec((SMALL_W, H * dk), lambda c: (0, 0)),
                  pl.BlockSpec((1, H * dk), lambda c: (0, 0)),
                  pl.BlockSpec((1, dv), lambda c: (0, 0))],
        out_specs=pl.BlockSpec((nb, chunk, H * dv), lambda c: (0, c, 0)),
        out_shape=jax.ShapeDtypeStruct((nb, seq, H * dv), BF16),
        scratch_shapes=[pltpu.VMEM((nb * H, dv, dk), F32)],
        compiler_params=_params("arbitrary"),
        name="gla",
    )(z3, zi3, z_meta, zi_meta, wlr_pad, blr, gn)


def _dsa_kernel(q_ref, zi_ref, ik_ref, relb_ref, km_hbm, z_hbm, vt_hbm, yb_ref,
                sc_ref, sb_ref, thr_ref, neg_ref, s_ref, cm_ref, m_ref, acc_ref, bias_ref,
                kbuf_ref, vbuf_ref, sem_ref, *, tq, topk):
    H, dh = DSA_HEADS, DSA_DH
    hd = H * dh
    b_id, qi = pl.program_id(0), pl.program_id(1)
    n_chunks = qi + 2
    krow = lax.broadcasted_iota(I32, (tq, tq), 0)
    qcol = lax.broadcasted_iota(I32, (tq, tq), 1)

    last_chunk = qi + 1

    def k_copy(c):
        slot = c % KV_SLOTS
        if isinstance(c, int) and c == 0:
            src = km_hbm
        else:
            start = (c - 1) * tq
            rows = pl.ds(start if isinstance(start, int) else pl.multiple_of(start, tq), tq)
            src = z_hbm.at[b_id, rows, pl.ds(ZG_W + hd, hd)]
        return pltpu.make_async_copy(src, kbuf_ref.at[slot], sem_ref.at[0, slot])

    def v_copy(c):
        slot = c % KV_SLOTS
        return pltpu.make_async_copy(vt_hbm.at[b_id, c], vbuf_ref.at[slot], sem_ref.at[1, slot])

    def kv_fetch(c):
        k_copy(c).start()
        v_copy(c).start()

    kv_fetch(0)
    kv_fetch(1)
    for c in range(2, min(KV_AHEAD, ik_ref.shape[1] - 1) + 1):
        pl.when(c <= last_chunk)(functools.partial(kv_fetch, c))

    @pl.when((b_id == 0) & (qi == 0))
    def _():
        for t, dist in enumerate((N_META + qcol - krow, qcol - krow, tq + qcol - krow)):
            for h in range(H):
                far = relb_ref[REL_BUCKETS - 1, h]
                tile = jnp.full((tq, tq), (relb_ref[0, h] - far) * LOG2E, F32)
                for bkt in range(1, REL_BUCKETS):
                    tile = jnp.where(dist >= BUCKET_START[bkt], (relb_ref[bkt, h] - far) * LOG2E, tile)
                bias_ref[t, h] = tile

    def score_and_select():
        zi = zi_ref[0]
        m_ref[...] = jnp.full(m_ref.shape, -1e30, F32)
        acc_ref[...] = jnp.zeros(acc_ref.shape, F32)

        iq = [zi[:, h * IDX_DH:(h + 1) * IDX_DH].astype(BF16) for h in range(IDX_HEADS)]
        small_t = zi[:, IQ_W:].T
        iw = [small_t[IW_OFF + h:IW_OFF + h + 1, :] for h in range(IDX_HEADS)]

        def scores(ci):
            ikc = ik_ref[0, ci]
            sc = jnp.zeros((tq, tq), F32)
            for h in range(IDX_HEADS):
                sc += iw[h] * jnp.maximum(_dot_nt(ikc, iq[h]), 0.0)
            return sc

        def store_scores(ci, sc):
            sc_ref[ci] = sc
            sb_ref[ci] = sc.astype(BF16)

        store_scores(0, jnp.where(krow < N_META, scores(0), -jnp.inf))

        def full_pair(i, carry):
            sa, sb = scores(2 * i + 1), scores(2 * i + 2)
            store_scores(2 * i + 1, sa)
            store_scores(2 * i + 2, sb)
            return carry

        lax.fori_loop(0, qi // 2, full_pair, 0)
        pl.when(qi % 2 == 1)(lambda: store_scores(qi, scores(qi)))
        store_scores(qi + 1, jnp.where(krow <= qcol, scores(qi + 1), -jnp.inf))
        store_scores(n_chunks, jnp.full((tq, tq), -jnp.inf, F32))
        n_pairs = (n_chunks + 1) // 2

        def as_float(key):
            return pltpu.bitcast(key ^ (lax.shift_right_arithmetic(key, 31) & 0x7FFFFFFF), F32)

        def count_rounded(cand):
            cand16 = jnp.broadcast_to(cand, (BF16_SUBLANES, tq)).astype(BF16)

            def body(i, accs):
                accs = list(accs)
                for ci in (2 * i, 2 * i + 1):
                    for j in range(tq // BF16_SUBLANES):
                        blk = sb_ref[ci, j * BF16_SUBLANES:(j + 1) * BF16_SUBLANES, :]
                        accs[j % len(accs)] = accs[j % len(accs)] + jnp.where(blk >= cand16, ONE16, ZERO16)
                return tuple(accs)

            accs = lax.fori_loop(0, n_pairs, body,
                                 tuple(jnp.zeros((BF16_SUBLANES, tq), jnp.int16) for _ in range(4)))
            tot = sum((a.astype(I32) for a in accs[1:]), accs[0].astype(I32))
            return jnp.sum(tot, axis=0, keepdims=True)

        def count(cand, strict=False):
            cand8 = jnp.broadcast_to(cand, (SUBLANES, tq))

            def body(i, accs):
                accs = list(accs)
                for ci in (2 * i, 2 * i + 1):
                    for j in range(tq // SUBLANES):
                        blk = sc_ref[ci, j * SUBLANES:(j + 1) * SUBLANES, :]
                        hit = blk > cand8 if strict else blk >= cand8
                        accs[j % len(accs)] = accs[j % len(accs)] + jnp.where(hit, 1, 0)
                return tuple(accs)

            accs = lax.fori_loop(0, n_pairs, body, tuple(jnp.zeros((SUBLANES, tq), I32) for _ in range(4)))
            return jnp.sum(sum(accs[1:], accs[0]), axis=0, keepdims=True)

        bias16 = 1 << (BF16_BITS - 1)

        def bf16_bits(k):
            return lax.shift_left(k ^ (lax.shift_right_arithmetic(k, BF16_BITS - 1) & (bias16 - 1)), BF16_BITS)

        def coarse(p, u):
            cand = u | lax.shift_left(jnp.int32(1), BF16_BITS - 1 - p)
            hit = count_rounded(pltpu.bitcast(bf16_bits(cand - bias16), F32)) >= topk
            return jnp.where(hit, cand, u)

        u = lax.fori_loop(0, BF16_BITS, coarse, jnp.zeros((1, tq), I32))
        none = u == 0
        low_bits = bf16_bits(u - bias16 - 1)
        low = low_bits ^ (lax.shift_right_arithmetic(low_bits, 31) & 0x7FFFFFFF)

        def fine(p, d):
            cand = d | lax.shift_left(jnp.int32(1), FINE_BITS - 1 - p)
            return jnp.where(count(as_float(low + cand)) >= topk, cand, d)

        d = lax.fori_loop(0, FINE_BITS, fine, jnp.zeros((1, tq), I32))
        thr = jnp.where(none, jnp.finfo(F32).min, as_float(low + d))
        thr_ref[...] = thr

        n_ge = jnp.where(none, 0, count(thr))

        @pl.when(jnp.max(n_ge) > topk)
        def _():
            need = (topk - count(thr, strict=True)).astype(F32)
            incl = (qcol <= krow).astype(BF16)

            def fix(ci, seen):
                sc = sc_ref[ci]
                tie = sc == thr
                tie_f = jnp.where(tie, 1.0, 0.0)
                run = jnp.dot(incl, tie_f.astype(BF16), preferred_element_type=F32)
                drop = tie & (seen + run > need)
                sc_ref[ci] = jnp.where(drop, -jnp.inf, sc)
                return seen + jnp.sum(tie_f, axis=0, keepdims=True)

            lax.fori_loop(0, n_chunks, fix, jnp.zeros((1, tq), F32))


    def logits_stage(c, par, h, bias_idx):
        hs = slice(h * dh, (h + 1) * dh)
        s = _dot_nt(kbuf_ref[c % KV_SLOTS, :, hs], q_ref[0, :, hs]) + neg_ref[...]
        if bias_idx is not None:
            s += bias_ref[bias_idx, h]
        s_ref[par, h] = s
        cm_ref[par, h] = jnp.max(s, axis=0, keepdims=True)

    def value_stage(c, par, h):
        m_old = m_ref[h]
        m_new = jnp.maximum(m_old, cm_ref[par, h])
        p = jnp.exp2(s_ref[par, h] - m_new).astype(BF16)
        pv = jnp.dot(vbuf_ref[c % KV_SLOTS, h * VROWS:(h + 1) * VROWS, :], p, preferred_element_type=F32)
        acc_ref[h] = jnp.exp2(m_old - m_new) * acc_ref[h] + pv
        m_ref[h] = m_new

    def pipeline_step(c, par, bias_idx, first=False):
        k_copy(c).wait()
        if not first:
            v_copy(c - 1).wait()
            pl.when(c + KV_AHEAD <= last_chunk)(functools.partial(kv_fetch, c + KV_AHEAD))
        neg_ref[...] = jnp.where(sc_ref[c] >= thr_ref[...], 0.0, -jnp.inf)
        for h in range(H):
            logits_stage(c, par, h, bias_idx)
            if not first:
                value_stage(c - 1, 1 - par, h)

    def drain(c, par):
        v_copy(c).wait()
        for h in range(H):
            value_stage(c, par, h)

    score_and_select()

    @pl.when(qi == 0)
    def _():
        pipeline_step(0, 0, 0, first=True)

    @pl.when(qi > 0)
    def _():
        pipeline_step(0, 0, None, first=True)

    def far_pair(i, carry):
        pipeline_step(2 * i + 1, 1, None)
        pipeline_step(2 * i + 2, 0, None)
        return carry

    lax.fori_loop(0, (qi - 1) // 2, far_pair, 0)
    qi_odd = qi % 2 == 1

    @pl.when((qi >= 2) & ~qi_odd)
    def _():
        pipeline_step(qi - 1, 1, None)

    @pl.when(qi_odd)
    def _():
        pipeline_step(qi, 1, 2)
        pipeline_step(qi + 1, 0, 1)
        drain(qi + 1, 0)

    @pl.when((qi >= 2) & ~qi_odd)
    def _():
        pipeline_step(qi, 0, 2)

    @pl.when(~qi_odd)
    def _():
        pipeline_step(qi + 1, 1, 1)
        drain(qi + 1, 1)

    for h in range(H):
        out_t = acc_ref[h, :dh, :] / acc_ref[h, dh:dh + 1, :]
        yb_ref[0, :, h * dh:(h + 1) * dh] = out_t.T.astype(yb_ref.dtype)


def _dsa(z3, vt, zi3, k_meta, ik, rel_bias, *, tq, topk):
    nb, seq, _ = z3.shape
    H, dh = DSA_HEADS, DSA_DH
    nq = seq // tq
    n_chunks = nq + 1
    hd = H * dh
    q_blk = ZG_W // hd
    hbm = pl.BlockSpec(memory_space=pl.ANY)
    return pl.pallas_call(
        functools.partial(_dsa_kernel, tq=tq, topk=topk),
        grid=(nb, nq),
        in_specs=[pl.BlockSpec((1, tq, hd), lambda b, i: (b, i, q_blk)),
                  pl.BlockSpec((1, tq, IQ_W + SMALL_W), lambda b, i: (b, i, 0)),
                  pl.BlockSpec((1, n_chunks, tq, IDX_DH), lambda b, i: (b, 0, 0, 0)),
                  pl.BlockSpec(memory_space=pltpu.SMEM),
                  hbm, hbm, hbm],
        out_specs=pl.BlockSpec((1, tq, hd), lambda b, i: (b, i, 0)),
        out_shape=jax.ShapeDtypeStruct((nb, seq, hd), BF16),
        scratch_shapes=[pltpu.VMEM((n_chunks + 1, tq, tq), F32),
                        pltpu.VMEM((n_chunks + 1, tq, tq), BF16),
                        pltpu.VMEM((1, tq), F32),
                        pltpu.VMEM((tq, tq), F32),
                        pltpu.VMEM((2, H, tq, tq), F32),
                        pltpu.VMEM((2, H, 1, tq), F32),
                        pltpu.VMEM((H, 1, tq), F32),
                        pltpu.VMEM((H, VROWS, tq), F32),
                        pltpu.VMEM((3, H, tq, tq), F32),
                        pltpu.VMEM((KV_SLOTS, tq, hd), BF16),
                        pltpu.VMEM((KV_SLOTS, H * VROWS, tq), BF16),
                        pltpu.SemaphoreType.DMA((2, KV_SLOTS))],
        compiler_params=_params("arbitrary", "arbitrary"),
        name="dsa",
    )(z3, zi3, ik, rel_bias, k_meta, z3, vt)


def _merge_kernel(h_ref, ya_ref, yb_ref, ga_ref, gb_ref, wa_ref, wb_ref, wo_ref, o_ref):
    ua = jnp.dot(ya_ref[...], wa_ref[...], preferred_element_type=F32)
    ub = jnp.dot(yb_ref[...], wb_ref[...], preferred_element_type=F32)
    merged = _sigmoid(ga_ref[...].astype(F32)) * ua + _sigmoid(gb_ref[...].astype(F32)) * ub
    o_ref[...] = h_ref[...] + jnp.dot(merged.astype(BF16), wo_ref[...], preferred_element_type=F32)


def _merge(h, ya, yb, z, wa, wb, wo, *, tm):
    n, d = h.shape
    gate_blk = (ZG_W + ZD_W) // d
    row = pl.BlockSpec((tm, d), lambda i: (i, 0))
    wspec = pl.BlockSpec((d, d), lambda i: (0, 0))
    return pl.pallas_call(
        _merge_kernel,
        grid=(n // tm,),
        in_specs=[row, row, row,
                  pl.BlockSpec((tm, d), lambda i: (i, gate_blk)),
                  pl.BlockSpec((tm, d), lambda i: (i, gate_blk + 1)),
                  wspec, wspec, wspec],
        out_specs=row,
        out_shape=jax.ShapeDtypeStruct((n, d), F32),
        compiler_params=_params("parallel"),
        name="merge",
    )(h, ya, yb, z, z, wa, wb, wo)


def kernel(x, meta_tokens, norm_ffn1, ffn1_w_gate, ffn1_w_up, ffn1_w_down, norm_mix, w_in, gla_w_lr, gla_b_lr,
           gla_norm, w_up_a, w_up_b, w_out, rel_bias, norm_ffn2, ffn2_w_gate, ffn2_w_up, ffn2_w_down, norm_final):
    nb, seq, d = x.shape
    assert norm_ffn1.shape[0] == 1, "single-layer block only"
    assert meta_tokens.shape[0] == N_META
    tq = 256
    assert seq % tq == 0 and (ZG_W + ZD_W) % d == 0 and ZG_W % (DSA_HEADS * DSA_DH) == 0
    topk = min(TOPK_MAX, seq // 4)
    dff = ffn1_w_gate.shape[2]
    tf = dff // 2
    bf = lambda a: a.astype(BF16)

    sizes = (GLA_HEADS * GLA_DK, GLA_HEADS * GLA_DK, GLA_HEADS * GLA_DV, GLA_HEADS * GLA_DV, GLA_LOWRANK,
             DSA_HEADS * DSA_DH, DSA_HEADS * DSA_DH, DSA_HEADS * DSA_DH, IQ_W, IDX_DH, IDX_HEADS, d, d)
    offs = np.concatenate([[0], np.cumsum(sizes)])
    (c_gq, c_gk, c_gv, c_gr, c_lr, c_dq, c_dk, c_dv, c_iq, c_ik, c_iw, c_ga, c_gb) = [
        w_in[0][:, offs[i]:offs[i + 1]] for i in range(len(sizes))]
    c_gq = c_gq * (GLA_DK ** -0.5)
    c_dq = c_dq * (DSA_DH ** -0.5 * LOG2E)
    c_iw = c_iw * ((IDX_HEADS ** -0.5) * (IDX_DH ** -0.5))
    w_big = bf(jnp.concatenate([c_gq, c_gk, c_gv, c_gr, c_dq, c_dk, c_ga, c_gb], axis=1))
    w_dv_t = bf(c_dv.T)
    pad_w = SMALL_W - (IDX_DH + IDX_HEADS + GLA_LOWRANK)
    w_small = bf(jnp.concatenate([c_iq, c_ik, c_iw, c_lr, jnp.zeros((d, pad_w), F32)], axis=1))
    wlr_pad = jnp.zeros((SMALL_W, GLA_HEADS * GLA_DK), F32).at[LR_OFF:LR_OFF + GLA_LOWRANK].set(gla_w_lr[0])

    xr = x.reshape(nb * seq, d)
    g1, gm = norm_ffn1, norm_mix
    wg1, wu1, wd1 = bf(ffn1_w_gate[0]), bf(ffn1_w_up[0]), bf(ffn1_w_down[0])

    h1 = _ffn(xr, g1, wg1, wu1, wd1, tm=1024, tf=tf)
    h1m = _ffn(meta_tokens, g1, wg1, wu1, wd1, tm=512, tf=tf)
    tn = w_big.shape[1] // 2
    z, zi = _proj(h1, gm, w_big, w_small, tm=1024, tn=tn, name="proj")
    zm, zim = _proj(h1m, gm, w_big, w_small, tm=1024, tn=tn, name="proj_meta")

    z3 = z.reshape(nb, seq, z.shape[1])
    zi3 = zi.reshape(nb, seq, zi.shape[1])

    ya = _gla(z3, zi3, zm, zim, wlr_pad, gla_b_lr, gla_norm, chunk=256)

    hd = DSA_HEADS * DSA_DH
    n_chunks = seq // tq + 1
    meta_chunk = lambda a: jnp.pad(a, ((0, tq - N_META), (0, 0)))
    with_meta = lambda m, xs: jnp.concatenate([jnp.broadcast_to(meta_chunk(m)[None], (nb, tq, m.shape[1])), xs], 1)
    k_meta = meta_chunk(zm[:, ZG_W + hd:ZG_W + 2 * hd])
    vt = _values_t(h1, meta_chunk(h1m), gm, w_dv_t, nb=nb, tq=tq)
    ik_all = with_meta(zim[:, IQ_W + IK_OFF:IQ_W + IK_OFF + IDX_DH], zi3[:, :, IQ_W + IK_OFF:IQ_W + IK_OFF + IDX_DH])
    ik = bf(ik_all.reshape(nb, n_chunks, tq, IDX_DH))
    yb = _dsa(z3, vt, zi3, k_meta, ik, rel_bias, tq=tq, topk=topk)

    h2 = _merge(h1, ya.reshape(nb * seq, -1), yb.reshape(nb * seq, -1), z,
                bf(w_up_a[0]), bf(w_up_b[0]), bf(w_out[0]), tm=1024)
    out = _ffn(h2, norm_ffn2, bf(ffn2_w_gate[0]), bf(ffn2_w_up[0]), bf(ffn2_w_down[0]), norm_final[None, :],
               tm=1024, tf=tf)
    return out.reshape(nb, seq, d)
```
